```python
import jax
import jax.numpy as jnp
from jax import lax
import numpy as np

D_MODEL = 1024
BATCH = 8
SEQ = 8192
DEPTH = 2

D_MIX = D_MODEL
HEAD_DIM = 64
ATTN_WIDTH = D_MIX // 4
ATTN_HEADS = ATTN_WIDTH // HEAD_DIM
DILATED_PATTERNS = ((128, 1), (512, 4), (2048, 16))
MASK_VALUE = -1e30
SSM_WIDTH = D_MIX // 2
SSM_HEADS = SSM_WIDTH // HEAD_DIM
SSM_GROUPS = 2
SSM_STATE = 128
SSM_CONV = 4
SSM_CHUNK = 128
SSM_BC = SSM_GROUPS * SSM_STATE
SSM_CONV_DIM = SSM_WIDTH + 2 * SSM_BC
HGRN_WIDTH = D_MIX // 4
HGRN_HEADS = HGRN_WIDTH // HEAD_DIM
HGRN_EXPAND = 64
HGRN_KEY_WIDTH = HGRN_HEADS * HGRN_EXPAND
HGRN_CHUNK = 64
HGRN_LB_FLOOR = 1e-20
D_FF = 256 * ((8 * D_MODEL // 3 + 255) // 256)
NORM_EPS = 1e-6
IN_SIZES = (ATTN_WIDTH, ATTN_WIDTH, ATTN_WIDTH,
            SSM_WIDTH, SSM_CONV_DIM, SSM_HEADS,
            HGRN_KEY_WIDTH, HGRN_KEY_WIDTH, HGRN_WIDTH, HGRN_WIDTH)
D_IN_PROJ = sum(IN_SIZES)

kernel_name = 'hybrid_dilated_ssd_hgrn2_macaron'


def _rms(x):
    xf = x.astype(jnp.float32)
    return xf * lax.rsqrt(jnp.mean(xf * xf, axis=-1, keepdims=True) + NORM_EPS)


def rms_norm(x, w):
    return (_rms(x) * w.astype(jnp.float32)).astype(x.dtype)


def swiglu(x, w_gate, w_up, w_down):
    return (jax.nn.silu(x @ w_gate) * (x @ w_up)) @ w_down


def _split_columns(p):
    out, start = [], 0
    for size in IN_SIZES:
        out.append(p[..., start:start + size])
        start += size
    return out


def _masked_exp(mask, logits):
    return jnp.where(mask, jnp.exp(jnp.where(mask, logits, 0.0)), 0.0)


def _dilated_window_branch(q, k, v, window, dilation):
    b, L, h, hd = q.shape
    blk = window // dilation
    span = blk * dilation
    Lp = -(-L // span) * span
    M = Lp // dilation
    nb = M // blk

    def split(t):
        t = jnp.pad(t, ((0, 0), (0, Lp - L), (0, 0), (0, 0))).reshape(b, M, dilation, h, hd)
        return t.transpose(0, 2, 3, 1, 4).reshape(b, dilation, h, nb, blk, hd)

    def with_prev(t):
        prev = jnp.pad(t, ((0, 0), (0, 0), (0, 0), (1, 0), (0, 0), (0, 0)))[:, :, :, :-1]
        return jnp.concatenate([prev, t], axis=4)

    qb = split(q).astype(jnp.float32)
    kk = with_prev(split(k)).astype(jnp.float32)
    vv = with_prev(split(v)).astype(jnp.float32)
    s = jnp.einsum('brhnqd,brhnkd->brhnqk', qb, kk) * (hd ** -0.5)
    qi = jnp.arange(blk)[:, None]
    ki = jnp.arange(2 * blk)[None, :]
    band = (ki >= qi) & (ki <= qi + blk)
    not_before_start = (jnp.arange(nb)[:, None, None] > 0) | (ki[None] >= blk)
    valid = band[None] & not_before_start
    s = jnp.where(valid, s, MASK_VALUE)
    m = jnp.max(s, axis=-1)
    p = jnp.where(valid, jnp.exp(s - m[..., None]), 0.0)
    l = jnp.sum(p, axis=-1)
    o = jnp.einsum('brhnqk,brhnkd->brhnqd', p, vv) / l[..., None]

    def merge(t):
        t = t.reshape((b, dilation, h, M) + t.shape[5:])
        t = jnp.moveaxis(t, (1, 2), (2, 3))
        return t.reshape((b, Lp, h) + t.shape[4:])[:, :L]

    return merge(o), merge(m), merge(l)


def dilated_attention(q, k, v):
    outs = [_dilated_window_branch(q, k, v, w, d) for (w, d) in DILATED_PATTERNS]
    m_all = jnp.stack([m for (_, m, _) in outs])
    l_all = jnp.stack([l for (_, _, l) in outs])
    o_all = jnp.stack([o for (o, _, _) in outs])
    wts = l_all * jnp.exp(m_all - jnp.max(m_all, axis=0, keepdims=True))
    o = jnp.sum(wts[..., None] * o_all, axis=0) / jnp.sum(wts, axis=0)[..., None]
    return o.astype(q.dtype)


def causal_depthwise_conv(x, w, bias):
    k_width, ch = w.shape
    y = lax.conv_general_dilated(x, w[:, None, :].astype(x.dtype), window_strides=(1,),
                                 padding=((k_width - 1, 0),),
                                 dimension_numbers=('NWC', 'WIO', 'NWC'),
                                 feature_group_count=ch)
    return y + bias.astype(x.dtype)


def ssd_chunked(x, a, Bm, Cm, chunk):
    b, L, g, e, p = x.shape
    n = Bm.shape[-1]
    nc = L // chunk
    xc = x.reshape(b, nc, chunk, g, e, p)
    Bc = Bm.reshape(b, nc, chunk, g, n)
    Cc = Cm.reshape(b, nc, chunk, g, n)
    a_cs = jnp.cumsum(a.reshape(b, nc, chunk, g, e).transpose(0, 1, 3, 4, 2), axis=-1)
    causal = jnp.tril(jnp.ones((chunk, chunk), dtype=bool))
    seg = a_cs[..., :, None] - a_cs[..., None, :]
    Lmat = _masked_exp(causal, seg)
    cb = jnp.einsum('bctgn,bcsgn->bcgts', Cc, Bc)
    y_diag = jnp.einsum('bcgts,bcgets,bcsgep->bctgep', cb, Lmat, xc)
    decay_states = jnp.exp(a_cs[..., -1:] - a_cs)
    states = jnp.einsum('bcsgn,bcges,bcsgep->bcgepn', Bc, decay_states, xc)
    chunk_decay = jnp.exp(a_cs[..., -1])

    def step(hstate, inp):
        st, dec = inp
        return dec[..., None, None] * hstate + st, hstate

    h0 = jnp.zeros((b, g, e, p, n), dtype=x.dtype)
    _, prev = lax.scan(step, h0, (jnp.moveaxis(states, 1, 0), jnp.moveaxis(chunk_decay, 1, 0)))
    prev = jnp.moveaxis(prev, 0, 1)
    y_off = jnp.einsum('bctgn,bcgepn,bcget->bctgep', Cc, prev, jnp.exp(a_cs))
    return (y_diag + y_off).reshape(b, L, g, e, p)


def mamba2_mixer(z, xbc, dt_raw, conv_w, conv_b, dt_bias, a_log, d_skip, norm_w):
    b, L, _ = xbc.shape
    e = SSM_HEADS // SSM_GROUPS
    hp = SSM_WIDTH // SSM_HEADS
    xbc = jax.nn.silu(causal_depthwise_conv(xbc, conv_w, conv_b)).astype(jnp.float32)
    xs = xbc[..., :SSM_WIDTH].reshape(b, L, SSM_GROUPS, e, hp)
    Bm = xbc[..., SSM_WIDTH:SSM_WIDTH + SSM_BC].reshape(b, L, SSM_GROUPS, SSM_STATE)
    Cm = xbc[..., SSM_WIDTH + SSM_BC:].reshape(b, L, SSM_GROUPS, SSM_STATE)
    dt = jax.nn.softplus(dt_raw.astype(jnp.float32) + dt_bias.astype(jnp.float32))
    dt = dt.reshape(b, L, SSM_GROUPS, e)
    A = -jnp.exp(a_log.astype(jnp.float32)).reshape(SSM_GROUPS, e)
    y = ssd_chunked(xs * dt[..., None], A * dt, Bm, Cm, SSM_CHUNK)
    y = y + d_skip.astype(jnp.float32).reshape(SSM_GROUPS, e)[:, :, None] * xs
    y = y.reshape(b, L, SSM_WIDTH) * jax.nn.silu(z.astype(jnp.float32))
    y = _rms(y.reshape(b, L, SSM_GROUPS, SSM_WIDTH // SSM_GROUPS)).reshape(b, L, SSM_WIDTH)
    return (y * norm_w.astype(jnp.float32)).astype(z.dtype)


def chunk_gated_recurrence(q, k, v, log_f, chunk):
    b, L, H, dk = q.shape
    dv = v.shape[-1]
    nc = L // chunk

    def to_chunks(t):
        return t.reshape(b, nc, chunk, H, t.shape[-1]).transpose(1, 0, 3, 2, 4)

    causal = jnp.tril(jnp.ones((chunk, chunk), dtype=bool))[:, :, None]

    def step(S, inp):
        qi, ki, vi, fi = inp
        bcum = jnp.cumsum(fi, axis=2)
        o_inter = jnp.einsum('bhtk,bhkv->bhtv', qi * jnp.exp(bcum), S)
        rel = bcum[:, :, :, None, :] - bcum[:, :, None, :, :]
        decay = _masked_exp(causal, rel)
        att = jnp.einsum('bhtk,bhtsk,bhsk->bhts', qi, decay, ki)
        o_intra = jnp.einsum('bhts,bhsv->bhtv', att, vi)
        blast = bcum[:, :, -1]
        k_dec = ki * jnp.exp(blast[:, :, None] - bcum)
        S = jnp.exp(blast)[..., None] * S + jnp.einsum('bhsk,bhsv->bhkv', k_dec, vi)
        return S, o_inter + o_intra

    S0 = jnp.zeros((b, H, dk, dv), dtype=q.dtype)
    _, o = lax.scan(step, S0, (to_chunks(q), to_chunks(k), to_chunks(v), to_chunks(log_f)))
    return o.transpose(1, 0, 3, 2, 4).reshape(b, L, H, dv)


def hgrn2_lower_bounds(lb_logits):
    sm = jax.nn.softmax(lb_logits.astype(jnp.float32), axis=0)
    return jnp.cumsum(sm, axis=0) - sm[0]


def hgrn2_mixer(hq, hf, hi, hg, lb, norm_w):
    b, L, _ = hq.shape
    dv = HGRN_WIDTH // HGRN_HEADS
    q = jax.nn.silu(hq.astype(jnp.float32)).reshape(b, L, HGRN_HEADS, HGRN_EXPAND)
    lb = jnp.clip(lb, HGRN_LB_FLOOR, 1.0 - 1e-6)
    log_f = jnp.logaddexp(jnp.log(lb), jnp.log1p(-lb) + jax.nn.log_sigmoid(hf.astype(jnp.float32)))
    log_f = log_f.reshape(b, L, HGRN_HEADS, HGRN_EXPAND)
    k = -jnp.expm1(log_f)
    v = hi.astype(jnp.float32).reshape(b, L, HGRN_HEADS, dv)
    o = chunk_gated_recurrence(q, k, v, log_f, HGRN_CHUNK)
    o = _rms(o).reshape(b, L, HGRN_WIDTH) * norm_w.astype(jnp.float32)
    return (o * jax.nn.silu(hg.astype(jnp.float32))).astype(hq.dtype)


def setup_inputs(seed: int = 0) -> dict:
    key = jax.random.key(seed)
    ks = jax.random.split(key, 24)

    def nrm(k, shape, scale):
        return jax.random.normal(k, shape, jnp.float32) * scale

    def gain(k, shape):
        return 1.0 + 0.05 * jax.random.normal(k, shape, jnp.float32)

    dt0 = jnp.exp(jax.random.uniform(ks[9], (DEPTH, SSM_HEADS), jnp.float32,
                                     np.log(1e-3).astype(np.float32), np.log(1e-1).astype(np.float32)))
    return {
        'x': nrm(ks[0], (BATCH, SEQ, D_MODEL), 1.0),
        'ffn1_norm': gain(ks[1], (DEPTH, D_MODEL)),
        'ffn1_w_gate': nrm(ks[2], (DEPTH, D_MODEL, D_FF), D_MODEL ** -0.5),
        'ffn1_w_up': nrm(ks[3], (DEPTH, D_MODEL, D_FF), D_MODEL ** -0.5),
        'ffn1_w_down': nrm(ks[4], (DEPTH, D_FF, D_MODEL), D_FF ** -0.5),
        'mix_norm': gain(ks[5], (DEPTH, D_MODEL)),
        'w_in': nrm(ks[6], (DEPTH, D_MODEL, D_IN_PROJ), D_MODEL ** -0.5),
        'conv_w': nrm(ks[7], (DEPTH, SSM_CONV, SSM_CONV_DIM), SSM_CONV ** -0.5),
        'conv_b': nrm(ks[8], (DEPTH, SSM_CONV_DIM), 0.02),
        'dt_bias': dt0 + jnp.log(-jnp.expm1(-dt0)),
        'a_log': jnp.log(jax.random.uniform(ks[10], (DEPTH, SSM_HEADS), jnp.float32, 1.0, 16.0)),
        'd_skip': gain(ks[11], (DEPTH, SSM_HEADS)),
        'ssm_norm': gain(ks[12], (DEPTH, SSM_WIDTH)),
        'hgrn_lb_logits': nrm(ks[13], (DEPTH, HGRN_KEY_WIDTH), 0.1),
        'hgrn_norm': gain(ks[14], (DEPTH, HGRN_WIDTH)),
        'w_out': nrm(ks[15], (DEPTH, D_MIX, D_MODEL), D_MIX ** -0.5),
        'ffn2_norm': gain(ks[16], (DEPTH, D_MODEL)),
        'ffn2_w_gate': nrm(ks[17], (DEPTH, D_MODEL, D_FF), D_MODEL ** -0.5),
        'ffn2_w_up': nrm(ks[18], (DEPTH, D_MODEL, D_FF), D_MODEL ** -0.5),
        'ffn2_w_down': nrm(ks[19], (DEPTH, D_FF, D_MODEL), D_FF ** -0.5),
        'final_norm': gain(ks[20], (D_MODEL,)),
    }


def reference(x, ffn1_norm, ffn1_w_gate, ffn1_w_up, ffn1_w_down, mix_norm, w_in, conv_w, conv_b,
              dt_bias, a_log, d_skip, ssm_norm, hgrn_lb_logits, hgrn_norm, w_out, ffn2_norm,
              ffn2_w_gate, ffn2_w_up, ffn2_w_down, final_norm):
    b, L, _ = x.shape
    lower_bounds = hgrn2_lower_bounds(hgrn_lb_logits)
    h = x
    for layer in range(DEPTH):
        h = h + 0.5 * swiglu(rms_norm(h, ffn1_norm[layer]), ffn1_w_gate[layer],
                             ffn1_w_up[layer], ffn1_w_down[layer])
        u = rms_norm(h, mix_norm[layer])
        aq, ak, av, z, xbc, dt_raw, hq, hf, hi, hg = _split_columns(u @ w_in[layer])
        heads = lambda t: t.reshape(b, L, ATTN_HEADS, HEAD_DIM)
        y_attn = dilated_attention(heads(aq), heads(ak), heads(av)).reshape(b, L, ATTN_WIDTH)
        y_ssm = mamba2_mixer(z, xbc, dt_raw, conv_w[layer], conv_b[layer], dt_bias[layer],
                             a_log[layer], d_skip[layer], ssm_norm[layer])
        y_hgrn = hgrn2_mixer(hq, hf, hi, hg, lower_bounds[layer], hgrn_norm[layer])
        y = jnp.concatenate([y_attn.astype(h.dtype), y_ssm.astype(h.dtype), y_hgrn.astype(h.dtype)], axis=-1)
        h = h + y @ w_out[layer]
        h = h + 0.5 * swiglu(rms_norm(h, ffn2_norm[layer]), ffn2_w_gate[layer],
                             ffn2_w_up[layer], ffn2_w_down[layer])
    return rms_norm(h, final_norm)
```

```python
import functools

import numpy as np
import jax
import jax.numpy as jnp
from jax import lax
from jax.experimental import pallas as pl
from jax.experimental.pallas import tpu as pltpu

F32 = jnp.float32
BF16 = jnp.bfloat16

D_MODEL = 1024
HEAD_DIM = 64
LANES = 128
ATTN_WIDTH = 256
DILATED_PATTERNS = ((128, 1), (512, 4), (2048, 16))
ATTN_BLK = 128
ATTN_TILE = 2048
MASK_VALUE = -1e30
SSM_WIDTH = 512
SSM_HEADS = 8
SSM_GROUPS = 2
SSM_STATE = 128
SSM_CONV = 4
SSM_CHUNK = 128
SSM_BC = SSM_GROUPS * SSM_STATE
SSM_TILE = 1024
HGRN_WIDTH = 256
HGRN_KEY_WIDTH = 256
HGRN_CHUNK = 64
HGRN_LEVELS = (32, 16, 8, 4, 2, 1)
HGRN_TILE = 1024
HGRN_LB_FLOOR = 1e-20
D_FF = 2816
NORM_EPS = 1e-6
TOKEN_TILE = 512
COPY_ROWS = 256
FF_CHUNK = 256
VMEM_LIMIT = 56 * 1024 * 1024

N_ATTN_SLABS = 6
N_SSD_SLABS = 13
N_HGRN_SLABS = 8
N_IN_SLABS = N_ATTN_SLABS + N_SSD_SLABS + N_HGRN_SLABS


def _silu(x):
    return x * jax.nn.sigmoid(x)


def _rms_scale(x, w):
    return x * lax.rsqrt(jnp.mean(x * x, axis=-1, keepdims=True) + NORM_EPS) * w


def _dot(a, b):
    return jnp.dot(a, b, preferred_element_type=F32)


def _dot_nt(a, b):
    return lax.dot_general(a, b, (((1,), (1,)), ((), ())), preferred_element_type=F32)


def _dot_tn(a, b):
    return lax.dot_general(a, b, (((0,), (0,)), ((), ())), preferred_element_type=F32)


def _exact_dot01(g, x):
    x1 = x.astype(BF16)
    r1 = x - x1.astype(F32)
    x2 = r1.astype(BF16)
    x3 = (r1 - x2.astype(F32)).astype(BF16)
    y = _dot(g, jnp.concatenate([x1, x2, x3], axis=1))
    return y[:, 0:LANES] + y[:, LANES:2 * LANES] + y[:, 2 * LANES:3 * LANES]


def _const_spec(shape):
    nd = len(shape)
    return pl.BlockSpec(shape, lambda *_: (0,) * nd, pipeline_mode=pl.Buffered(1))


def _ffn_body(x_ref, nw_ref, wg_ref, wu_ref, wd_ref, fw_ref, o_ref, a_ref, *, final):
    x = x_ref[...]
    xn = _rms_scale(x, nw_ref[...]).astype(BF16)
    for c in range(D_FF // FF_CHUNK):
        sl = slice(c * FF_CHUNK, (c + 1) * FF_CHUNK)
        g = _dot(xn, wg_ref[:, sl])
        u = _dot(xn, wu_ref[:, sl])
        a_ref[:, sl] = (_silu(g) * u).astype(BF16)
    y = x + 0.5 * _dot(a_ref[...], wd_ref[...])
    if final:
        y = _rms_scale(y, fw_ref[...])
    o_ref[...] = y


def _ffn_call(h, nw, wg, wu, wd, fw, final):
    n = h.shape[0]
    tm = TOKEN_TILE
    return pl.pallas_call(
        functools.partial(_ffn_body, final=final),
        grid=(n // tm,),
        in_specs=[
            pl.BlockSpec((tm, D_MODEL), lambda i: (i, 0)),
            _const_spec((1, D_MODEL)),
            _const_spec((D_MODEL, D_FF)),
            _const_spec((D_MODEL, D_FF)),
            _const_spec((D_FF, D_MODEL)),
            _const_spec((1, D_MODEL)),
        ],
        out_specs=pl.BlockSpec((tm, D_MODEL), lambda i: (i, 0)),
        out_shape=jax.ShapeDtypeStruct((n, D_MODEL), F32),
        scratch_shapes=[pltpu.VMEM((tm, D_FF), BF16)],
        compiler_params=pltpu.CompilerParams(
            dimension_semantics=("arbitrary",), vmem_limit_bytes=VMEM_LIMIT),
        name="ffn",
    )(h, nw, wg, wu, wd, fw)


def _inproj_body(x_ref, nw_ref, w_ref, attn_ref, ssd_ref, hgrn_ref):
    xn = _rms_scale(x_ref[...], nw_ref[...]).astype(BF16)

    def put(slab, val):
        if slab < N_ATTN_SLABS:
            attn_ref[slab // 2, slab % 2] = val
        elif slab < N_ATTN_SLABS + N_SSD_SLABS:
            ssd_ref[slab - N_ATTN_SLABS] = val
        else:
            hgrn_ref[slab - N_ATTN_SLABS - N_SSD_SLABS] = val

    for s0 in range(0, N_IN_SLABS, 2):
        s1 = min(s0 + 2, N_IN_SLABS)
        res = _dot(xn, w_ref[:, s0 * LANES:s1 * LANES])
        for s in range(s0, s1):
            put(s, res[:, (s - s0) * LANES:(s - s0 + 1) * LANES])


def _inproj_call(h, nw, w):
    n = h.shape[0]
    tm = TOKEN_TILE
    return pl.pallas_call(
        _inproj_body,
        grid=(n // tm,),
        in_specs=[
            pl.BlockSpec((tm, D_MODEL), lambda i: (i, 0)),
            _const_spec((1, D_MODEL)),
            _const_spec((D_MODEL, N_IN_SLABS * LANES)),
        ],
        out_specs=[
            pl.BlockSpec((3, 2, tm, LANES), lambda i: (0, 0, i, 0)),
            pl.BlockSpec((N_SSD_SLABS, tm, LANES), lambda i: (0, i, 0)),
            pl.BlockSpec((N_HGRN_SLABS, tm, LANES), lambda i: (0, i, 0)),
        ],
        out_shape=[
            jax.ShapeDtypeStruct((3, 2, n, LANES), F32),
            jax.ShapeDtypeStruct((N_SSD_SLABS, n, LANES), F32),
            jax.ShapeDtypeStruct((N_HGRN_SLABS, n, LANES), F32),
        ],
        compiler_params=pltpu.CompilerParams(
            dimension_semantics=("arbitrary",), vmem_limit_bytes=VMEM_LIMIT),
        name="inproj",
    )(h, nw, w)


def _attn_block(qb, kk, vv, valid, h0):
    qs = qb * (HEAD_DIM ** -0.5)
    kb = kk.astype(BF16)
    vb = vv.astype(BF16)
    res = []
    for q_h in (jnp.where(h0, qs, 0.0), jnp.where(h0, 0.0, qs)):
        s = _dot_nt(q_h.astype(BF16), kb)
        s = jnp.where(valid, s, MASK_VALUE)
        m = jnp.max(s, axis=-1, keepdims=True)
        p = jnp.where(valid, jnp.exp(s - m), 0.0)
        l = jnp.sum(p, axis=-1, keepdims=True)
        o = _dot(p.astype(BF16), vb)
        res.append((o, m, l))
    (o0, m0, l0), (o1, m1, l1) = res
    return jnp.where(h0, o0, o1), jnp.where(h0, m0, m1), jnp.where(h0, l0, l1)


def _attn_body(in_ref, out_ref, kv_ref, acc_ref, m_ref, l_ref):
    T = ATTN_TILE
    j = pl.program_id(1)

    @pl.when(j == 0)
    def _():
        def zero_prev(i, carry):
            rows = pl.ds(pl.multiple_of(i * COPY_ROWS, COPY_ROWS), COPY_ROWS)
            for t in range(2):
                for p in range(2):
                    kv_ref[t, p, rows, :] = jnp.zeros((COPY_ROWS, LANES), F32)
            return carry

        lax.fori_loop(0, T // COPY_ROWS, zero_prev, 0)

    def load_kv(i, carry):
        rows = pl.ds(pl.multiple_of(i * COPY_ROWS, COPY_ROWS), COPY_ROWS)
        dst = pl.ds(pl.multiple_of(T + i * COPY_ROWS, COPY_ROWS), COPY_ROWS)
        for t in range(2):
            for p in range(2):
                kv_ref[t, p, dst, :] = in_ref[t, p, rows, :]
        return carry

    lax.fori_loop(0, T // COPY_ROWS, load_kv, 0)

    h0 = lax.broadcasted_iota(jnp.int32, (1, LANES), 1) < HEAD_DIM
    qi = lax.broadcasted_iota(jnp.int32, (ATTN_BLK, 2 * ATTN_BLK), 0)
    ki = lax.broadcasted_iota(jnp.int32, (ATTN_BLK, 2 * ATTN_BLK), 1)
    band = (ki >= qi) & (ki <= qi + ATTN_BLK)

    def run_block(pair, q_idx, k_idx, first, fresh):
        acc_v, m_v, l_v = acc_ref.at[pair], m_ref.at[pair], l_ref.at[pair]
        qb = in_ref.at[2, pair][q_idx, :]
        kk = kv_ref.at[0, pair][k_idx, :]
        vv = kv_ref.at[1, pair][k_idx, :]
        valid = band & (ki >= jnp.where(first, ATTN_BLK, 0))
        o, m, l = _attn_block(qb, kk, vv, valid, h0)
        if not fresh:
            m_old = m_v[q_idx, :]
            m_new = jnp.maximum(m_old, m)
            a = jnp.exp(m_old - m_new)
            b = jnp.exp(m - m_new)
            o = a * acc_v[q_idx, :] + b * o
            l = a * l_v[q_idx, :] + b * l
            m = m_new
        acc_v[q_idx, :] = o
        m_v[q_idx, :] = m
        l_v[q_idx, :] = l

    for pair in range(2):
        def p1(n, carry):
            r0 = pl.multiple_of(n * ATTN_BLK, ATTN_BLK)
            run_block(pair, pl.ds(r0, ATTN_BLK), pl.ds(r0 + T - ATTN_BLK, 2 * ATTN_BLK),
                      (j == 0) & (n == 0), True)
            return carry

        lax.fori_loop(0, T // ATTN_BLK, p1, 0)

        def p2(idx, carry):
            c = lax.shift_right_logical(idx, 2)
            r = idx & 3
            run_block(pair, pl.ds(c * 512 + r, ATTN_BLK, stride=4),
                      pl.ds(T + (c - 1) * 512 + r, 2 * ATTN_BLK, stride=4),
                      (j == 0) & (c == 0), False)
            return carry

        lax.fori_loop(0, (T // 512) * 4, p2, 0)

        def p3(r, carry):
            run_block(pair, pl.ds(r, ATTN_BLK, stride=16), pl.ds(r, 2 * ATTN_BLK, stride=16),
                      j == 0, False)
            return carry

        lax.fori_loop(0, 16, p3, 0)

    def finish(i, carry):
        rows = pl.ds(pl.multiple_of(i * COPY_ROWS, COPY_ROWS), COPY_ROWS)
        src = pl.ds(pl.multiple_of(T + i * COPY_ROWS, COPY_ROWS), COPY_ROWS)
        for p in range(2):
            out_ref[p, rows, :] = acc_ref[p, rows, :] / l_ref[p, rows, :]
            for t in range(2):
                kv_ref[t, p, rows, :] = kv_ref[t, p, src, :]
        return carry

    lax.fori_loop(0, T // COPY_ROWS, finish, 0)


def _attn_call(attn_in, batch, seq):
    T = ATTN_TILE
    nc = seq // T
    n = batch * seq
    return pl.pallas_call(
        _attn_body,
        grid=(batch, nc),
        in_specs=[pl.BlockSpec((3, 2, T, LANES), lambda b, j: (0, 0, b * nc + j, 0))],
        out_specs=pl.BlockSpec((2, T, LANES), lambda b, j: (0, b * nc + j, 0)),
        out_shape=jax.ShapeDtypeStruct((2, n, LANES), F32),
        scratch_shapes=[
            pltpu.VMEM((2, 2, 2 * T, LANES), F32),
            pltpu.VMEM((2, T, LANES), F32),
            pltpu.VMEM((2, T, LANES), F32),
            pltpu.VMEM((2, T, LANES), F32),
        ],
        compiler_params=pltpu.CompilerParams(
            dimension_semantics=("arbitrary", "arbitrary"), vmem_limit_bytes=VMEM_LIMIT),
        name="dilated_attn",
    )(attn_in)


def _softplus(x):
    return jnp.maximum(x, 0.0) + jnp.log1p(jnp.exp(-jnp.abs(x)))


def _ssd_body(in_ref, cw_ref, cb_ref, dtb_ref, alog_ref, dsk_ref, nw_ref, tri_ref, out_ref,
              xpad_ref, xc_ref, st_ref):
    T = SSM_TILE
    C = SSM_CHUNK
    j = pl.program_id(1)

    @pl.when(j == 0)
    def _():
        st_ref[...] = jnp.zeros_like(st_ref)
        xpad_ref[:, 0:8, :] = jnp.zeros((8, 8, LANES), F32)

    @pl.when(j > 0)
    def _():
        xpad_ref[:, 5:8, :] = xpad_ref[:, T + 5:T + 8, :]

    def conv(s, carry):
        xpad_ref[s, 8:T + 8, :] = in_ref[4 + s]
        acc = cb_ref[pl.ds(s, 1), :] + cw_ref[0, pl.ds(s, 1), :] * xpad_ref[s, 5:5 + T, :]
        for t in range(1, SSM_CONV):
            acc = acc + cw_ref[t, pl.ds(s, 1), :] * xpad_ref[s, 5 + t:5 + t + T, :]
        xc_ref[s] = _silu(acc)
        return carry

    lax.fori_loop(0, 8, conv, 0)

    h0 = lax.broadcasted_iota(jnp.int32, (1, LANES), 1) < HEAD_DIM
    ti = lax.broadcasted_iota(jnp.int32, (C, C), 0)
    si = lax.broadcasted_iota(jnp.int32, (C, C), 1)
    causal = si <= ti
    a_neg = -jnp.exp(alog_ref[...])

    def chunk(c, carry):
        rows = pl.ds(pl.multiple_of(c * C, C), C)
        dt = _softplus(in_ref[12, rows, :] + dtb_ref[...])
        acs = _exact_dot01(tri_ref[...], dt * a_neg)
        dt_t = dt.T
        acs_t = acs.T
        ys = [None] * 4
        for g in range(SSM_GROUPS):
            bm = xc_ref[4 + g, rows, :]
            cm = xc_ref[6 + g, rows, :]
            bm_t = bm.T
            cmb = cm.astype(BF16)
            cb = _dot(cmb, bm_t.astype(BF16))
            for sl in (2 * g, 2 * g + 1):
                xs = xc_ref[sl, rows, :]
                xb = xs.astype(BF16)
                yd, stn, ecol, cdec = [], [], [], []
                for h in (2 * sl, 2 * sl + 1):
                    arow = acs_t[h:h + 1, :]
                    acol = acs[:, h:h + 1]
                    dtrow = dt_t[h:h + 1, :]
                    alast = acs_t[h:h + 1, C - 1:C]
                    seg = jnp.where(causal, acol - arow, 0.0)
                    lmat = jnp.where(causal, jnp.exp(seg), 0.0)
                    yd.append(_dot((cb * lmat * dtrow).astype(BF16), xb))
                    wrow = jnp.exp(alast - arow) * dtrow
                    stn.append(_dot((bm_t * wrow).astype(BF16), xb))
                    ecol.append(jnp.exp(acol))
                    cdec.append(jnp.exp(alast))
                st_old = st_ref[sl]
                y_off = _dot(cmb, st_old.astype(BF16)) * jnp.where(h0, ecol[0], ecol[1])
                st_ref[sl] = st_old * jnp.where(h0, cdec[0], cdec[1]) + jnp.where(h0, stn[0], stn[1])
                y = jnp.where(h0, yd[0], yd[1]) + y_off + dsk_ref[sl:sl + 1, :] * xs
                ys[sl] = y * _silu(in_ref[sl, rows, :])
        for g in range(SSM_GROUPS):
            ya, yb = ys[2 * g], ys[2 * g + 1]
            ss = jnp.sum(ya * ya, axis=-1, keepdims=True) + jnp.sum(yb * yb, axis=-1, keepdims=True)
            r = lax.rsqrt(ss * (1.0 / (2 * LANES)) + NORM_EPS)
            out_ref[2 * g, rows, :] = ya * r * nw_ref[2 * g:2 * g + 1, :]
            out_ref[2 * g + 1, rows, :] = yb * r * nw_ref[2 * g + 1:2 * g + 2, :]
        return carry

    lax.fori_loop(0, T // C, chunk, 0)


def _ssd_call(ssd_in, cw, cb, dtb, alog, dsk, nw, tri, batch, seq):
    T = SSM_TILE
    nt = seq // T
    n = batch * seq
    return pl.pallas_call(
        _ssd_body,
        grid=(batch, nt),
        in_specs=[
            pl.BlockSpec((N_SSD_SLABS, T, LANES), lambda b, j: (0, b * nt + j, 0)),
            _const_spec((SSM_CONV, 8, LANES)),
            _const_spec((8, LANES)),
            _const_spec((1, LANES)),
            _const_spec((1, LANES)),
            _const_spec((4, LANES)),
            _const_spec((4, LANES)),
            _const_spec((SSM_CHUNK, SSM_CHUNK)),
        ],
        out_specs=pl.BlockSpec((4, T, LANES), lambda b, j: (0, b * nt + j, 0)),
        out_shape=jax.ShapeDtypeStruct((4, n, LANES), F32),
        scratch_shapes=[
            pltpu.VMEM((8, T + 8, LANES), F32),
            pltpu.VMEM((8, T, LANES), F32),
            pltpu.VMEM((4, SSM_STATE, LANES), F32),
        ],
        compiler_params=pltpu.CompilerParams(
            dimension_semantics=("arbitrary", "arbitrary"), vmem_limit_bytes=VMEM_LIMIT),
        name="ssd",
    )(ssd_in, cw, cb, dtb, alog, dsk, nw, tri)


def _hgrn_gather_matrix():
    C = HGRN_CHUNK
    tri = np.tril(np.ones((C, C), np.float32))
    mats = [tri]
    t = np.arange(C)
    for m in HGRN_LEVELS:
        ref = (t // (2 * m)) * (2 * m) + m - 1
        mats.append(tri[ref])
    return np.concatenate(mats, axis=0)


def _hgrn_body(in_ref, loglb_ref, log1mlb_ref, nw_ref, gt_ref, bd_ref, out_ref, st_ref):
    T = HGRN_TILE
    C = HGRN_CHUNK
    j = pl.program_id(1)

    @pl.when(j == 0)
    def _():
        st_ref[...] = jnp.zeros_like(st_ref)

    h0 = lax.broadcasted_iota(jnp.int32, (1, LANES), 1) < HEAD_DIM
    t_col = lax.broadcasted_iota(jnp.int32, (C, 1), 0)
    t2 = lax.broadcasted_iota(jnp.int32, (2 * C, C), 0) & (C - 1)
    s2 = lax.broadcasted_iota(jnp.int32, (2 * C, C), 1)
    bd = bd_ref[...]
    bd_mask = bd > 0

    def chunk(c, carry):
        rows = pl.ds(pl.multiple_of(c * C, C), C)
        for sp in range(2):
            q = _silu(in_ref[sp, rows, :])
            hf = in_ref[2 + sp, rows, :]
            v = in_ref[4 + sp, rows, :]
            vb = v.astype(BF16)
            log_sig = jnp.minimum(hf, 0.0) - jnp.log1p(jnp.exp(-jnp.abs(hf)))
            a1 = loglb_ref[sp:sp + 1, :]
            a2 = log1mlb_ref[sp:sp + 1, :] + log_sig
            lf = jnp.maximum(a1, a2) + jnp.log1p(jnp.exp(-jnp.abs(a1 - a2)))
            kk = jnp.exp(log1mlb_ref[sp:sp + 1, :]) * jax.nn.sigmoid(-hf)
            r_all = _exact_dot01(gt_ref[...], lf)
            b = r_all[0:C]
            blast = b[C - 1:C, :]
            st_old = st_ref[sp]
            o = _dot_nt((q * jnp.exp(b)).astype(BF16), st_old.astype(BF16))
            kd = (kk * jnp.exp(blast - b)).astype(BF16)
            upd = _dot_tn(vb, kd)
            st_ref[sp] = st_old * jnp.exp(blast) + jnp.where(bd_mask, upd, 0.0)
            att = jnp.zeros((2 * C, C), F32)
            for i, m in enumerate(HGRN_LEVELS):
                rm = r_all[(i + 1) * C:(i + 2) * C]
                second = (lax.shift_right_logical(t_col, m.bit_length() - 1) & 1) == 1
                em = jnp.exp(-jnp.abs(b - rm))
                qs = jnp.where(second, q * em, 0.0)
                ks = jnp.where(second, 0.0, kk * em)
                lhs = jnp.concatenate([jnp.where(h0, qs, 0.0), jnp.where(h0, 0.0, qs)], axis=0)
                p = _dot_nt(lhs.astype(BF16), ks.astype(BF16))
                same = (lax.shift_right_logical(t2, m.bit_length())
                        == lax.shift_right_logical(s2, m.bit_length()))
                att = att + jnp.where(same, p, 0.0)
            oi = _dot(att.astype(BF16), vb)
            o = o + jnp.where(h0, oi[0:C], oi[C:2 * C])
            o = o + _dot((q * kk).astype(BF16), bd) * v
            o2 = o * o
            ss0 = jnp.sum(jnp.where(h0, o2, 0.0), axis=-1, keepdims=True)
            ss1 = jnp.sum(jnp.where(h0, 0.0, o2), axis=-1, keepdims=True)
            r = lax.rsqrt(jnp.where(h0, ss0, ss1) * (1.0 / HEAD_DIM) + NORM_EPS)
            out_ref[sp, rows, :] = o * r * nw_ref[sp:sp + 1, :] * _silu(in_ref[6 + sp, rows, :])
        return carry

    lax.fori_loop(0, T // C, chunk, 0)


def _hgrn_call(hgrn_in, loglb, log1mlb, nw, gt, bd, batch, seq):
    T = HGRN_TILE
    nt = seq // T
    n = batch * seq
    return pl.pallas_call(
        _hgrn_body,
        grid=(batch, nt),
        in_specs=[
            pl.BlockSpec((N_HGRN_SLABS, T, LANES), lambda b, j: (0, b * nt + j, 0)),
            _const_spec((2, LANES)),
            _const_spec((2, LANES)),
            _const_spec((2, LANES)),
            _const_spec(((1 + len(HGRN_LEVELS)) * HGRN_CHUNK, HGRN_CHUNK)),
            _const_spec((LANES, LANES)),
        ],
        out_specs=pl.BlockSpec((2, T, LANES), lambda b, j: (0, b * nt + j, 0)),
        out_shape=jax.ShapeDtypeStruct((2, n, LANES), F32),
        scratch_shapes=[pltpu.VMEM((2, LANES, LANES), F32)],
        compiler_params=pltpu.CompilerParams(
            dimension_semantics=("arbitrary", "arbitrary"), vmem_limit_bytes=VMEM_LIMIT),
        name="hgrn2",
    )(hgrn_in, loglb, log1mlb, nw, gt, bd)


def _outproj_body(h_ref, ya_ref, ys_ref, yh_ref, w_ref, o_ref):
    parts = [ya_ref[0], ya_ref[1], ys_ref[0], ys_ref[1], ys_ref[2], ys_ref[3], yh_ref[0], yh_ref[1]]
    y = jnp.concatenate([p.astype(BF16) for p in parts], axis=1)
    o_ref[...] = h_ref[...] + _dot(y, w_ref[...])


def _outproj_call(h, ya, ys, yh, w):
    n = h.shape[0]
    tm = TOKEN_TILE
    return pl.pallas_call(
        _outproj_body,
        grid=(n // tm,),
        in_specs=[
            pl.BlockSpec((tm, D_MODEL), lambda i: (i, 0)),
            pl.BlockSpec((2, tm, LANES), lambda i: (0, i, 0)),
            pl.BlockSpec((4, tm, LANES), lambda i: (0, i, 0)),
            pl.BlockSpec((2, tm, LANES), lambda i: (0, i, 0)),
            _const_spec((D_MODEL, D_MODEL)),
        ],
        out_specs=pl.BlockSpec((tm, D_MODEL), lambda i: (i, 0)),
        out_shape=jax.ShapeDtypeStruct((n, D_MODEL), F32),
        compiler_params=pltpu.CompilerParams(
            dimension_semantics=("arbitrary",), vmem_limit_bytes=VMEM_LIMIT),
        name="outproj",
    )(h, ya, ys, yh, w)


def _permute_w_in(w):
    o = 0
    cols = {}
    for name, size in (("aq", 256), ("ak", 256), ("av", 256), ("z", 512), ("x", 512), ("B", 256),
                       ("C", 256), ("dt", 8), ("hq", 256), ("hf", 256), ("hi", 256), ("hg", 256)):
        cols[name] = w[:, o:o + size]
        o += size
    dt = jnp.pad(cols["dt"], ((0, 0), (0, LANES - SSM_HEADS)))
    order = ["ak", "av", "aq", "z", "x", "B", "C"]
    return jnp.concatenate([cols[k] for k in order] + [dt] + [cols[k] for k in ("hq", "hf", "hi", "hg")],
                           axis=1)


def _pad_lanes(v):
    return jnp.pad(v, (0, LANES - v.shape[0])).reshape(1, LANES)


def kernel(x, ffn1_norm, ffn1_w_gate, ffn1_w_up, ffn1_w_down, mix_norm, w_in, conv_w, conv_b, dt_bias, a_log, d_skip, ssm_norm, hgrn_lb_logits, hgrn_norm, w_out, ffn2_norm, ffn2_w_gate, ffn2_w_up, ffn2_w_down, final_norm):
    batch, seq, d = x.shape
    depth = w_in.shape[0]
    assert d == D_MODEL and seq % ATTN_TILE == 0
    h = x.reshape(batch * seq, d)

    sm = jax.nn.softmax(hgrn_lb_logits.astype(F32), axis=0)
    lower = jnp.clip(jnp.cumsum(sm, axis=0) - sm[0], HGRN_LB_FLOOR, 1.0 - 1e-6)
    tri = jnp.asarray(np.tril(np.ones((SSM_CHUNK, SSM_CHUNK), np.float32)), BF16)
    gt = jnp.asarray(_hgrn_gather_matrix(), BF16)
    bd = jnp.asarray(np.kron(np.eye(2, dtype=np.float32), np.ones((HEAD_DIM, HEAD_DIM), np.float32)), BF16)
    fw = final_norm.reshape(1, d)

    for layer in range(depth):
        h = _ffn_call(h, ffn1_norm[layer].reshape(1, d), ffn1_w_gate[layer].astype(BF16),
                      ffn1_w_up[layer].astype(BF16), ffn1_w_down[layer].astype(BF16), fw, False)
        attn_in, ssd_in, hgrn_in = _inproj_call(h, mix_norm[layer].reshape(1, d),
                                                _permute_w_in(w_in[layer]).astype(BF16))
        y_attn = _attn_call(attn_in, batch, seq)
        y_ssm = _ssd_call(
            ssd_in,
            conv_w[layer].reshape(SSM_CONV, 8, LANES), conv_b[layer].reshape(8, LANES),
            _pad_lanes(dt_bias[layer]), _pad_lanes(a_log[layer]),
            jnp.repeat(d_skip[layer], HEAD_DIM).reshape(4, LANES), ssm_norm[layer].reshape(4, LANES),
            tri, batch, seq)
        lb = lower[layer]
        y_hgrn = _hgrn_call(hgrn_in, jnp.log(lb).reshape(2, LANES), jnp.log1p(-lb).reshape(2, LANES),
                            hgrn_norm[layer].reshape(2, LANES), gt, bd, batch, seq)
        h = _outproj_call(h, y_attn, y_ssm, y_hgrn, w_out[layer].astype(BF16))
        h = _ffn_call(h, ffn2_norm[layer].reshape(1, d), ffn2_w_gate[layer].astype(BF16),
                      ffn2_w_up[layer].astype(BF16), ffn2_w_down[layer].astype(BF16), fw,
                      layer == depth - 1)
    return h.reshape(batch, seq, d)
```

```python
import functools

import numpy as np
import jax
import jax.numpy as jnp
from jax import lax
from jax.experimental import pallas as pl
from jax.experimental.pallas import tpu as pltpu

F32 = jnp.float32
BF16 = jnp.bfloat16

D_MODEL = 1024
HEAD_DIM = 64
LANES = 128
ATTN_WIDTH = 256
DILATED_PATTERNS = ((128, 1), (512, 4), (2048, 16))
ATTN_BLK = 128
ATTN_TILE = 2048
MASK_VALUE = -1e30
SSM_WIDTH = 512
SSM_HEADS = 8
SSM_GROUPS = 2
SSM_STATE = 128
SSM_CONV = 4
SSM_CHUNK = 128
SSM_BC = SSM_GROUPS * SSM_STATE
SSM_TILE = 1024
HGRN_WIDTH = 256
HGRN_KEY_WIDTH = 256
HGRN_CHUNK = 64
HGRN_LEVELS = (32, 16, 8, 4, 2, 1)
HGRN_TILE = 1024
HGRN_LB_FLOOR = 1e-20
D_FF = 2816
NORM_EPS = 1e-6
TOKEN_TILE = 512
COPY_ROWS = 256
ATTN_UNROLL = 4
MIXER_UNROLL = 2
FF_CHUNK = 256
VMEM_LIMIT = 56 * 1024 * 1024

N_ATTN_SLABS = 6
N_SSD_SLABS = 13
N_HGRN_SLABS = 8
N_IN_SLABS = N_ATTN_SLABS + N_SSD_SLABS + N_HGRN_SLABS


def _silu(x):
    return x * jax.nn.sigmoid(x)


def _rms_scale(x, w):
    return x * lax.rsqrt(jnp.mean(x * x, axis=-1, keepdims=True) + NORM_EPS) * w


def _dot(a, b):
    return jnp.dot(a, b, preferred_element_type=F32)


def _dot_nt(a, b):
    return lax.dot_general(a, b, (((1,), (1,)), ((), ())), preferred_element_type=F32)


def _dot_tn(a, b):
    return lax.dot_general(a, b, (((0,), (0,)), ((), ())), preferred_element_type=F32)


def _exact_dot01(g, x):
    x1 = x.astype(BF16)
    r1 = x - x1.astype(F32)
    x2 = r1.astype(BF16)
    x3 = (r1 - x2.astype(F32)).astype(BF16)
    y = _dot(g, jnp.concatenate([x1, x2, x3], axis=1))
    return y[:, 0:LANES] + y[:, LANES:2 * LANES] + y[:, 2 * LANES:3 * LANES]


def _const_spec(shape):
    nd = len(shape)
    return pl.BlockSpec(shape, lambda *_: (0,) * nd, pipeline_mode=pl.Buffered(1))


def _ffn_body(x_ref, nw_ref, wg_ref, wu_ref, wd_ref, fw_ref, o_ref, a_ref, *, final):
    x = x_ref[...]
    xn = _rms_scale(x, nw_ref[...]).astype(BF16)
    for c in range(D_FF // FF_CHUNK):
        sl = slice(c * FF_CHUNK, (c + 1) * FF_CHUNK)
        g = _dot(xn, wg_ref[:, sl])
        u = _dot(xn, wu_ref[:, sl])
        a_ref[:, sl] = (_silu(g) * u).astype(BF16)
    y = x + 0.5 * _dot(a_ref[...], wd_ref[...])
    if final:
        y = _rms_scale(y, fw_ref[...])
    o_ref[...] = y


def _ffn_call(h, nw, wg, wu, wd, fw, final):
    n = h.shape[0]
    tm = TOKEN_TILE
    return pl.pallas_call(
        functools.partial(_ffn_body, final=final),
        grid=(n // tm,),
        in_specs=[
            pl.BlockSpec((tm, D_MODEL), lambda i: (i, 0)),
            _const_spec((1, D_MODEL)),
            _const_spec((D_MODEL, D_FF)),
            _const_spec((D_MODEL, D_FF)),
            _const_spec((D_FF, D_MODEL)),
            _const_spec((1, D_MODEL)),
        ],
        out_specs=pl.BlockSpec((tm, D_MODEL), lambda i: (i, 0)),
        out_shape=jax.ShapeDtypeStruct((n, D_MODEL), F32),
        scratch_shapes=[pltpu.VMEM((tm, D_FF), BF16)],
        compiler_params=pltpu.CompilerParams(
            dimension_semantics=("arbitrary",), vmem_limit_bytes=VMEM_LIMIT),
        name="ffn",
    )(h, nw, wg, wu, wd, fw)


def _inproj_body(x_ref, nw_ref, w_ref, attn_ref, ssd_ref, hgrn_ref):
    xn = _rms_scale(x_ref[...], nw_ref[...]).astype(BF16)

    def put(slab, val):
        if slab < N_ATTN_SLABS:
            attn_ref[slab // 2, slab % 2] = val
        elif slab < N_ATTN_SLABS + N_SSD_SLABS:
            ssd_ref[slab - N_ATTN_SLABS] = val
        else:
            hgrn_ref[slab - N_ATTN_SLABS - N_SSD_SLABS] = val

    for s0 in range(0, N_IN_SLABS, 2):
        s1 = min(s0 + 2, N_IN_SLABS)
        res = _dot(xn, w_ref[:, s0 * LANES:s1 * LANES])
        for s in range(s0, s1):
            put(s, res[:, (s - s0) * LANES:(s - s0 + 1) * LANES])


def _inproj_call(h, nw, w):
    n = h.shape[0]
    tm = TOKEN_TILE
    return pl.pallas_call(
        _inproj_body,
        grid=(n // tm,),
        in_specs=[
            pl.BlockSpec((tm, D_MODEL), lambda i: (i, 0)),
            _const_spec((1, D_MODEL)),
            _const_spec((D_MODEL, N_IN_SLABS * LANES)),
        ],
        out_specs=[
            pl.BlockSpec((3, 2, tm, LANES), lambda i: (0, 0, i, 0)),
            pl.BlockSpec((N_SSD_SLABS, tm, LANES), lambda i: (0, i, 0)),
            pl.BlockSpec((N_HGRN_SLABS, tm, LANES), lambda i: (0, i, 0)),
        ],
        out_shape=[
            jax.ShapeDtypeStruct((3, 2, n, LANES), F32),
            jax.ShapeDtypeStruct((N_SSD_SLABS, n, LANES), F32),
            jax.ShapeDtypeStruct((N_HGRN_SLABS, n, LANES), F32),
        ],
        compiler_params=pltpu.CompilerParams(
            dimension_semantics=("arbitrary",), vmem_limit_bytes=VMEM_LIMIT),
        name="inproj",
    )(h, nw, w)


def _attn_block(qb, kk, vv, valid, h0):
    qs = qb * (HEAD_DIM ** -0.5)
    kb = kk.astype(BF16)
    vb = vv.astype(BF16)
    res = []
    for q_h in (jnp.where(h0, qs, 0.0), jnp.where(h0, 0.0, qs)):
        s = _dot_nt(q_h.astype(BF16), kb)
        s = jnp.where(valid, s, MASK_VALUE)
        m = jnp.max(s, axis=-1, keepdims=True)
        p = jnp.where(valid, jnp.exp(s - m), 0.0)
        l = jnp.sum(p, axis=-1, keepdims=True)
        o = _dot(p.astype(BF16), vb)
        res.append((o, m, l))
    (o0, m0, l0), (o1, m1, l1) = res
    return jnp.where(h0, o0, o1), jnp.where(h0, m0, m1), jnp.where(h0, l0, l1)


def _attn_body(in_ref, out_ref, kv_ref, acc_ref, m_ref, l_ref):
    T = ATTN_TILE
    j = pl.program_id(1)

    @pl.when(j == 0)
    def _():
        def zero_prev(i, carry):
            rows = pl.ds(pl.multiple_of(i * COPY_ROWS, COPY_ROWS), COPY_ROWS)
            for t in range(2):
                for p in range(2):
                    kv_ref[t, p, rows, :] = jnp.zeros((COPY_ROWS, LANES), F32)
            return carry

        lax.fori_loop(0, T // COPY_ROWS, zero_prev, 0)

    def load_kv(i, carry):
        rows = pl.ds(pl.multiple_of(i * COPY_ROWS, COPY_ROWS), COPY_ROWS)
        dst = pl.ds(pl.multiple_of(T + i * COPY_ROWS, COPY_ROWS), COPY_ROWS)
        for t in range(2):
            for p in range(2):
                kv_ref[t, p, dst, :] = in_ref[t, p, rows, :]
        return carry

    lax.fori_loop(0, T // COPY_ROWS, load_kv, 0)

    h0 = lax.broadcasted_iota(jnp.int32, (1, LANES), 1) < HEAD_DIM
    qi = lax.broadcasted_iota(jnp.int32, (ATTN_BLK, 2 * ATTN_BLK), 0)
    ki = lax.broadcasted_iota(jnp.int32, (ATTN_BLK, 2 * ATTN_BLK), 1)
    band = (ki >= qi) & (ki <= qi + ATTN_BLK)

    def run_block(pair, q_idx, k_idx, first, fresh):
        acc_v, m_v, l_v = acc_ref.at[pair], m_ref.at[pair], l_ref.at[pair]
        qb = in_ref.at[2, pair][q_idx, :]
        kk = kv_ref.at[0, pair][k_idx, :]
        vv = kv_ref.at[1, pair][k_idx, :]
        valid = band & (ki >= jnp.where(first, ATTN_BLK, 0))
        o, m, l = _attn_block(qb, kk, vv, valid, h0)
        if not fresh:
            m_old = m_v[q_idx, :]
            m_new = jnp.maximum(m_old, m)
            a = jnp.exp(m_old - m_new)
            b = jnp.exp(m - m_new)
            o = a * acc_v[q_idx, :] + b * o
            l = a * l_v[q_idx, :] + b * l
            m = m_new
        acc_v[q_idx, :] = o
        m_v[q_idx, :] = m
        l_v[q_idx, :] = l

    for pair in range(2):
        def p1(n, carry):
            r0 = pl.multiple_of(n * ATTN_BLK, ATTN_BLK)
            run_block(pair, pl.ds(r0, ATTN_BLK), pl.ds(r0 + T - ATTN_BLK, 2 * ATTN_BLK),
                      (j == 0) & (n == 0), True)
            return carry

        lax.fori_loop(0, T // ATTN_BLK, p1, 0, unroll=ATTN_UNROLL)

        def p2(idx, carry):
            c = lax.shift_right_logical(idx, 2)
            r = idx & 3
            run_block(pair, pl.ds(c * 512 + r, ATTN_BLK, stride=4),
                      pl.ds(T + (c - 1) * 512 + r, 2 * ATTN_BLK, stride=4),
                      (j == 0) & (c == 0), False)
            return carry

        lax.fori_loop(0, (T // 512) * 4, p2, 0, unroll=ATTN_UNROLL)

        def p3(r, carry):
            run_block(pair, pl.ds(r, ATTN_BLK, stride=16), pl.ds(r, 2 * ATTN_BLK, stride=16),
                      j == 0, False)
            return carry

        lax.fori_loop(0, 16, p3, 0, unroll=ATTN_UNROLL)

    def finish(i, carry):
        rows = pl.ds(pl.multiple_of(i * COPY_ROWS, COPY_ROWS), COPY_ROWS)
        src = pl.ds(pl.multiple_of(T + i * COPY_ROWS, COPY_ROWS), COPY_ROWS)
        for p in range(2):
            out_ref[p, rows, :] = acc_ref[p, rows, :] / l_ref[p, rows, :]
            for t in range(2):
                kv_ref[t, p, rows, :] = kv_ref[t, p, src, :]
        return carry

    lax.fori_loop(0, T // COPY_ROWS, finish, 0)


def _attn_call(attn_in, batch, seq):
    T = ATTN_TILE
    nc = seq // T
    n = batch * seq
    return pl.pallas_call(
        _attn_body,
        grid=(batch, nc),
        in_specs=[pl.BlockSpec((3, 2, T, LANES), lambda b, j: (0, 0, b * nc + j, 0))],
        out_specs=pl.BlockSpec((2, T, LANES), lambda b, j: (0, b * nc + j, 0)),
        out_shape=jax.ShapeDtypeStruct((2, n, LANES), F32),
        scratch_shapes=[
            pltpu.VMEM((2, 2, 2 * T, LANES), F32),
            pltpu.VMEM((2, T, LANES), F32),
            pltpu.VMEM((2, T, LANES), F32),
            pltpu.VMEM((2, T, LANES), F32),
        ],
        compiler_params=pltpu.CompilerParams(
            dimension_semantics=("arbitrary", "arbitrary"), vmem_limit_bytes=VMEM_LIMIT),
        name="dilated_attn",
    )(attn_in)


def _softplus(x):
    return jnp.maximum(x, 0.0) + jnp.log1p(jnp.exp(-jnp.abs(x)))


def _ssd_body(in_ref, cw_ref, cb_ref, dtb_ref, alog_ref, dsk_ref, nw_ref, tri_ref, out_ref,
              xpad_ref, xc_ref, st_ref):
    T = SSM_TILE
    C = SSM_CHUNK
    j = pl.program_id(1)

    @pl.when(j == 0)
    def _():
        st_ref[...] = jnp.zeros_like(st_ref)
        xpad_ref[:, 0:8, :] = jnp.zeros((8, 8, LANES), F32)

    @pl.when(j > 0)
    def _():
        xpad_ref[:, 5:8, :] = xpad_ref[:, T + 5:T + 8, :]

    def conv(s, carry):
        xpad_ref[s, 8:T + 8, :] = in_ref[4 + s]
        acc = cb_ref[pl.ds(s, 1), :] + cw_ref[0, pl.ds(s, 1), :] * xpad_ref[s, 5:5 + T, :]
        for t in range(1, SSM_CONV):
            acc = acc + cw_ref[t, pl.ds(s, 1), :] * xpad_ref[s, 5 + t:5 + t + T, :]
        xc_ref[s] = _silu(acc)
        return carry

    lax.fori_loop(0, 8, conv, 0)

    h0 = lax.broadcasted_iota(jnp.int32, (1, LANES), 1) < HEAD_DIM
    ti = lax.broadcasted_iota(jnp.int32, (C, C), 0)
    si = lax.broadcasted_iota(jnp.int32, (C, C), 1)
    causal = si <= ti
    a_neg = -jnp.exp(alog_ref[...])

    def chunk(c, carry):
        rows = pl.ds(pl.multiple_of(c * C, C), C)
        dt = _softplus(in_ref[12, rows, :] + dtb_ref[...])
        acs = _exact_dot01(tri_ref[...], dt * a_neg)
        dt_t = dt.T
        acs_t = acs.T
        ys = [None] * 4
        for g in range(SSM_GROUPS):
            bm = xc_ref[4 + g, rows, :]
            cm = xc_ref[6 + g, rows, :]
            bm_t = bm.T
            cmb = cm.astype(BF16)
            cb = _dot(cmb, bm_t.astype(BF16))
            for sl in (2 * g, 2 * g + 1):
                xs = xc_ref[sl, rows, :]
                xb = xs.astype(BF16)
                yd, stn, ecol, cdec = [], [], [], []
                for h in (2 * sl, 2 * sl + 1):
                    arow = acs_t[h:h + 1, :]
                    acol = acs[:, h:h + 1]
                    dtrow = dt_t[h:h + 1, :]
                    alast = acs_t[h:h + 1, C - 1:C]
                    seg = jnp.where(causal, acol - arow, 0.0)
                    lmat = jnp.where(causal, jnp.exp(seg), 0.0)
                    yd.append(_dot((cb * lmat * dtrow).astype(BF16), xb))
                    wrow = jnp.exp(alast - arow) * dtrow
                    stn.append(_dot((bm_t * wrow).astype(BF16), xb))
                    ecol.append(jnp.exp(acol))
                    cdec.append(jnp.exp(alast))
                st_old = st_ref[sl]
                y_off = _dot(cmb, st_old.astype(BF16)) * jnp.where(h0, ecol[0], ecol[1])
                st_ref[sl] = st_old * jnp.where(h0, cdec[0], cdec[1]) + jnp.where(h0, stn[0], stn[1])
                y = jnp.where(h0, yd[0], yd[1]) + y_off + dsk_ref[sl:sl + 1, :] * xs
                ys[sl] = y * _silu(in_ref[sl, rows, :])
        for g in range(SSM_GROUPS):
            ya, yb = ys[2 * g], ys[2 * g + 1]
            ss = jnp.sum(ya * ya, axis=-1, keepdims=True) + jnp.sum(yb * yb, axis=-1, keepdims=True)
            r = lax.rsqrt(ss * (1.0 / (2 * LANES)) + NORM_EPS)
            out_ref[2 * g, rows, :] = ya * r * nw_ref[2 * g:2 * g + 1, :]
            out_ref[2 * g + 1, rows, :] = yb * r * nw_ref[2 * g + 1:2 * g + 2, :]
        return carry

    lax.fori_loop(0, T // C, chunk, 0, unroll=MIXER_UNROLL)


def _ssd_call(ssd_in, cw, cb, dtb, alog, dsk, nw, tri, batch, seq):
    T = SSM_TILE
    nt = seq // T
    n = batch * seq
    return pl.pallas_call(
        _ssd_body,
        grid=(batch, nt),
        in_specs=[
            pl.BlockSpec((N_SSD_SLABS, T, LANES), lambda b, j: (0, b * nt + j, 0)),
            _const_spec((SSM_CONV, 8, LANES)),
            _const_spec((8, LANES)),
            _const_spec((1, LANES)),
            _const_spec((1, LANES)),
            _const_spec((4, LANES)),
            _const_spec((4, LANES)),
            _const_spec((SSM_CHUNK, SSM_CHUNK)),
        ],
        out_specs=pl.BlockSpec((4, T, LANES), lambda b, j: (0, b * nt + j, 0)),
        out_shape=jax.ShapeDtypeStruct((4, n, LANES), F32),
        scratch_shapes=[
            pltpu.VMEM((8, T + 8, LANES), F32),
            pltpu.VMEM((8, T, LANES), F32),
            pltpu.VMEM((4, SSM_STATE, LANES), F32),
        ],
        compiler_params=pltpu.CompilerParams(
            dimension_semantics=("arbitrary", "arbitrary"), vmem_limit_bytes=VMEM_LIMIT),
        name="ssd",
    )(ssd_in, cw, cb, dtb, alog, dsk, nw, tri)


def _hgrn_gather_matrix():
    C = HGRN_CHUNK
    tri = np.tril(np.ones((C, C), np.float32))
    mats = [tri]
    t = np.arange(C)
    for m in HGRN_LEVELS:
        ref = (t // (2 * m)) * (2 * m) + m - 1
        mats.append(tri[ref])
    return np.concatenate(mats, axis=0)


def _hgrn_body(in_ref, loglb_ref, log1mlb_ref, nw_ref, gt_ref, bd_ref, out_ref, st_ref):
    T = HGRN_TILE
    C = HGRN_CHUNK
    j = pl.program_id(1)

    @pl.when(j == 0)
    def _():
        st_ref[...] = jnp.zeros_like(st_ref)

    h0 = lax.broadcasted_iota(jnp.int32, (1, LANES), 1) < HEAD_DIM
    t_col = lax.broadcasted_iota(jnp.int32, (C, 1), 0)
    t2 = lax.broadcasted_iota(jnp.int32, (2 * C, C), 0) & (C - 1)
    s2 = lax.broadcasted_iota(jnp.int32, (2 * C, C), 1)
    bd = bd_ref[...]
    bd_mask = bd > 0

    def chunk(c, carry):
        rows = pl.ds(pl.multiple_of(c * C, C), C)
        for sp in range(2):
            q = _silu(in_ref[sp, rows, :])
            hf = in_ref[2 + sp, rows, :]
            v = in_ref[4 + sp, rows, :]
            vb = v.astype(BF16)
            log_sig = jnp.minimum(hf, 0.0) - jnp.log1p(jnp.exp(-jnp.abs(hf)))
            a1 = loglb_ref[sp:sp + 1, :]
            a2 = log1mlb_ref[sp:sp + 1, :] + log_sig
            lf = jnp.maximum(a1, a2) + jnp.log1p(jnp.exp(-jnp.abs(a1 - a2)))
            kk = jnp.exp(log1mlb_ref[sp:sp + 1, :]) * jax.nn.sigmoid(-hf)
            r_all = _exact_dot01(gt_ref[...], lf)
            b = r_all[0:C]
            blast = b[C - 1:C, :]
            st_old = st_ref[sp]
            o = _dot_nt((q * jnp.exp(b)).astype(BF16), st_old.astype(BF16))
            kd = (kk * jnp.exp(blast - b)).astype(BF16)
            upd = _dot_tn(vb, kd)
            st_ref[sp] = st_old * jnp.exp(blast) + jnp.where(bd_mask, upd, 0.0)
            att = jnp.zeros((2 * C, C), F32)
            for i, m in enumerate(HGRN_LEVELS):
                rm = r_all[(i + 1) * C:(i + 2) * C]
                second = (lax.shift_right_logical(t_col, m.bit_length() - 1) & 1) == 1
                em = jnp.exp(-jnp.abs(b - rm))
                qs = jnp.where(second, q * em, 0.0)
                ks = jnp.where(second, 0.0, kk * em)
                lhs = jnp.concatenate([jnp.where(h0, qs, 0.0), jnp.where(h0, 0.0, qs)], axis=0)
                p = _dot_nt(lhs.astype(BF16), ks.astype(BF16))
                same = (lax.shift_right_logical(t2, m.bit_length())
                        == lax.shift_right_logical(s2, m.bit_length()))
                att = att + jnp.where(same, p, 0.0)
            oi = _dot(att.astype(BF16), vb)
            o = o + jnp.where(h0, oi[0:C], oi[C:2 * C])
            o = o + _dot((q * kk).astype(BF16), bd) * v
            o2 = o * o
            ss0 = jnp.sum(jnp.where(h0, o2, 0.0), axis=-1, keepdims=True)
            ss1 = jnp.sum(jnp.where(h0, 0.0, o2), axis=-1, keepdims=True)
            r = lax.rsqrt(jnp.where(h0, ss0, ss1) * (1.0 / HEAD_DIM) + NORM_EPS)
            out_ref[sp, rows, :] = o * r * nw_ref[sp:sp + 1, :] * _silu(in_ref[6 + sp, rows, :])
        return carry

    lax.fori_loop(0, T // C, chunk, 0, unroll=MIXER_UNROLL)


def _hgrn_call(hgrn_in, loglb, log1mlb, nw, gt, bd, batch, seq):
    T = HGRN_TILE
    nt = seq // T
    n = batch * seq
    return pl.pallas_call(
        _hgrn_body,
        grid=(batch, nt),
        in_specs=[
            pl.BlockSpec((N_HGRN_SLABS, T, LANES), lambda b, j: (0, b * nt + j, 0)),
            _const_spec((2, LANES)),
            _const_spec((2, LANES)),
            _const_spec((2, LANES)),
            _const_spec(((1 + len(HGRN_LEVELS)) * HGRN_CHUNK, HGRN_CHUNK)),
            _const_spec((LANES, LANES)),
        ],
        out_specs=pl.BlockSpec((2, T, LANES), lambda b, j: (0, b * nt + j, 0)),
        out_shape=jax.ShapeDtypeStruct((2, n, LANES), F32),
        scratch_shapes=[pltpu.VMEM((2, LANES, LANES), F32)],
        compiler_params=pltpu.CompilerParams(
            dimension_semantics=("arbitrary", "arbitrary"), vmem_limit_bytes=VMEM_LIMIT),
        name="hgrn2",
    )(hgrn_in, loglb, log1mlb, nw, gt, bd)


def _outproj_body(h_ref, ya_ref, ys_ref, yh_ref, w_ref, o_ref):
    parts = [ya_ref[0], ya_ref[1], ys_ref[0], ys_ref[1], ys_ref[2], ys_ref[3], yh_ref[0], yh_ref[1]]
    y = jnp.concatenate([p.astype(BF16) for p in parts], axis=1)
    o_ref[...] = h_ref[...] + _dot(y, w_ref[...])


def _outproj_call(h, ya, ys, yh, w):
    n = h.shape[0]
    tm = TOKEN_TILE
    return pl.pallas_call(
        _outproj_body,
        grid=(n // tm,),
        in_specs=[
            pl.BlockSpec((tm, D_MODEL), lambda i: (i, 0)),
            pl.BlockSpec((2, tm, LANES), lambda i: (0, i, 0)),
            pl.BlockSpec((4, tm, LANES), lambda i: (0, i, 0)),
            pl.BlockSpec((2, tm, LANES), lambda i: (0, i, 0)),
            _const_spec((D_MODEL, D_MODEL)),
        ],
        out_specs=pl.BlockSpec((tm, D_MODEL), lambda i: (i, 0)),
        out_shape=jax.ShapeDtypeStruct((n, D_MODEL), F32),
        compiler_params=pltpu.CompilerParams(
            dimension_semantics=("arbitrary",), vmem_limit_bytes=VMEM_LIMIT),
        name="outproj",
    )(h, ya, ys, yh, w)


def _permute_w_in(w):
    o = 0
    cols = {}
    for name, size in (("aq", 256), ("ak", 256), ("av", 256), ("z", 512), ("x", 512), ("B", 256),
                       ("C", 256), ("dt", 8), ("hq", 256), ("hf", 256), ("hi", 256), ("hg", 256)):
        cols[name] = w[:, o:o + size]
        o += size
    dt = jnp.pad(cols["dt"], ((0, 0), (0, LANES - SSM_HEADS)))
    order = ["ak", "av", "aq", "z", "x", "B", "C"]
    return jnp.concatenate([cols[k] for k in order] + [dt] + [cols[k] for k in ("hq", "hf", "hi", "hg")],
                           axis=1)


def _pad_lanes(v):
    return jnp.pad(v, (0, LANES - v.shape[0])).reshape(1, LANES)


def kernel(x, ffn1_norm, ffn1_w_gate, ffn1_w_up, ffn1_w_down, mix_norm, w_in, conv_w, conv_b, dt_bias, a_log, d_skip, ssm_norm, hgrn_lb_logits, hgrn_norm, w_out, ffn2_norm, ffn2_w_gate, ffn2_w_up, ffn2_w_down, final_norm):
    batch, seq, d = x.shape
    depth = w_in.shape[0]
    assert d == D_MODEL and seq % ATTN_TILE == 0
    h = x.reshape(batch * seq, d)

    sm = jax.nn.softmax(hgrn_lb_logits.astype(F32), axis=0)
    lower = jnp.clip(jnp.cumsum(sm, axis=0) - sm[0], HGRN_LB_FLOOR, 1.0 - 1e-6)
    tri = jnp.asarray(np.tril(np.ones((SSM_CHUNK, SSM_CHUNK), np.float32)), BF16)
    gt = jnp.asarray(_hgrn_gather_matrix(), BF16)
    bd = jnp.asarray(np.kron(np.eye(2, dtype=np.float32), np.ones((HEAD_DIM, HEAD_DIM), np.float32)), BF16)
    fw = final_norm.reshape(1, d)

    for layer in range(depth):
        h = _ffn_call(h, ffn1_norm[layer].reshape(1, d), ffn1_w_gate[layer].astype(BF16),
                      ffn1_w_up[layer].astype(BF16), ffn1_w_down[layer].astype(BF16), fw, False)
        attn_in, ssd_in, hgrn_in = _inproj_call(h, mix_norm[layer].reshape(1, d),
                                                _permute_w_in(w_in[layer]).astype(BF16))
        y_attn = _attn_call(attn_in, batch, seq)
        y_ssm = _ssd_call(
            ssd_in,
            conv_w[layer].reshape(SSM_CONV, 8, LANES), conv_b[layer].reshape(8, LANES),
            _pad_lanes(dt_bias[layer]), _pad_lanes(a_log[layer]),
            jnp.repeat(d_skip[layer], HEAD_DIM).reshape(4, LANES), ssm_norm[layer].reshape(4, LANES),
            tri, batch, seq)
        lb = lower[layer]
        y_hgrn = _hgrn_call(hgrn_in, jnp.log(lb).reshape(2, LANES), jnp.log1p(-lb).reshape(2, LANES),
                            hgrn_norm[layer].reshape(2, LANES), gt, bd, batch, seq)
        h = _outproj_call(h, y_attn, y_ssm, y_hgrn, w_out[layer].astype(BF16))
        h = _ffn_call(h, ffn2_norm[layer].reshape(1, d), ffn2_w_gate[layer].astype(BF16),
                      ffn2_w_up[layer].astype(BF16), ffn2_w_down[layer].astype(BF16), fw,
                      layer == depth - 1)
    return h.reshape(batch, seq, d)
```

```python
import functools

import numpy as np
import jax
import jax.numpy as jnp
from jax import lax
from jax.experimental import pallas as pl
from jax.experimental.pallas import tpu as pltpu

F32 = jnp.float32
BF16 = jnp.bfloat16

D_MODEL = 1024
HEAD_DIM = 64
LANES = 128
ATTN_WIDTH = 256
DILATED_PATTERNS = ((128, 1), (512, 4), (2048, 16))
ATTN_BLK = 128
ATTN_TILE = 2048
MASK_VALUE = -1e30
SSM_WIDTH = 512
SSM_HEADS = 8
SSM_GROUPS = 2
SSM_STATE = 128
SSM_CONV = 4
SSM_CHUNK = 128
SSM_BC = SSM_GROUPS * SSM_STATE
SSM_TILE = 1024
HGRN_WIDTH = 256
HGRN_KEY_WIDTH = 256
HGRN_CHUNK = 64
HGRN_LEVELS = (32, 16, 8, 4, 2, 1)
HGRN_TILE = 1024
HGRN_LB_FLOOR = 1e-20
D_FF = 2816
NORM_EPS = 1e-6
TOKEN_TILE = 512
COPY_ROWS = 256
ATTN_UNROLL = 4
MIXER_UNROLL = 2
HGRN_UNROLL = 4
FF_CHUNK = 256
VMEM_LIMIT = 56 * 1024 * 1024

N_ATTN_SLABS = 6
N_SSD_SLABS = 13
N_HGRN_SLABS = 8
N_IN_SLABS = N_ATTN_SLABS + N_SSD_SLABS + N_HGRN_SLABS


def _silu(x):
    return x * jax.nn.sigmoid(x)


def _silu_tanh(x):
    return x * (0.5 + 0.5 * jnp.tanh(0.5 * x))


def _log1p_exp_neg_abs(x):
    return jnp.log(1.0 + jnp.exp(-jnp.abs(x)))


def _rms_scale(x, w):
    return x * lax.rsqrt(jnp.mean(x * x, axis=-1, keepdims=True) + NORM_EPS) * w


def _dot(a, b):
    return jnp.dot(a, b, preferred_element_type=F32)


def _dot_nt(a, b):
    return lax.dot_general(a, b, (((1,), (1,)), ((), ())), preferred_element_type=F32)


def _dot_tn(a, b):
    return lax.dot_general(a, b, (((0,), (0,)), ((), ())), preferred_element_type=F32)


def _exact_dot01(g, x):
    x1 = x.astype(BF16)
    r1 = x - x1.astype(F32)
    x2 = r1.astype(BF16)
    x3 = (r1 - x2.astype(F32)).astype(BF16)
    y = _dot(g, jnp.concatenate([x1, x2, x3], axis=1))
    return y[:, 0:LANES] + y[:, LANES:2 * LANES] + y[:, 2 * LANES:3 * LANES]


def _const_spec(shape):
    nd = len(shape)
    return pl.BlockSpec(shape, lambda *_: (0,) * nd, pipeline_mode=pl.Buffered(1))


def _ffn_body(x_ref, nw_ref, wg_ref, wu_ref, wd_ref, fw_ref, o_ref, a_ref, *, final):
    x = x_ref[...]
    xn = _rms_scale(x, nw_ref[...]).astype(BF16)
    for c in range(D_FF // FF_CHUNK):
        sl = slice(c * FF_CHUNK, (c + 1) * FF_CHUNK)
        g = _dot(xn, wg_ref[:, sl])
        u = _dot(xn, wu_ref[:, sl])
        a_ref[:, sl] = (_silu(g) * u).astype(BF16)
    y = x + 0.5 * _dot(a_ref[...], wd_ref[...])
    if final:
        y = _rms_scale(y, fw_ref[...])
    o_ref[...] = y


def _ffn_call(h, nw, wg, wu, wd, fw, final):
    n = h.shape[0]
    tm = TOKEN_TILE
    return pl.pallas_call(
        functools.partial(_ffn_body, final=final),
        grid=(n // tm,),
        in_specs=[
            pl.BlockSpec((tm, D_MODEL), lambda i: (i, 0)),
            _const_spec((1, D_MODEL)),
            _const_spec((D_MODEL, D_FF)),
            _const_spec((D_MODEL, D_FF)),
            _const_spec((D_FF, D_MODEL)),
            _const_spec((1, D_MODEL)),
        ],
        out_specs=pl.BlockSpec((tm, D_MODEL), lambda i: (i, 0)),
        out_shape=jax.ShapeDtypeStruct((n, D_MODEL), F32),
        scratch_shapes=[pltpu.VMEM((tm, D_FF), BF16)],
        compiler_params=pltpu.CompilerParams(
            dimension_semantics=("arbitrary",), vmem_limit_bytes=VMEM_LIMIT),
        name="ffn",
    )(h, nw, wg, wu, wd, fw)


def _inproj_body(x_ref, nw_ref, w_ref, attn_ref, ssd_ref, hgrn_ref):
    xn = _rms_scale(x_ref[...], nw_ref[...]).astype(BF16)

    def put(slab, val):
        if slab < N_ATTN_SLABS:
            attn_ref[slab // 2, slab % 2] = val
        elif slab < N_ATTN_SLABS + N_SSD_SLABS:
            ssd_ref[slab - N_ATTN_SLABS] = val
        else:
            hgrn_ref[slab - N_ATTN_SLABS - N_SSD_SLABS] = val

    for s0 in range(0, N_IN_SLABS, 2):
        s1 = min(s0 + 2, N_IN_SLABS)
        res = _dot(xn, w_ref[:, s0 * LANES:s1 * LANES])
        for s in range(s0, s1):
            put(s, res[:, (s - s0) * LANES:(s - s0 + 1) * LANES])


def _inproj_call(h, nw, w):
    n = h.shape[0]
    tm = TOKEN_TILE
    return pl.pallas_call(
        _inproj_body,
        grid=(n // tm,),
        in_specs=[
            pl.BlockSpec((tm, D_MODEL), lambda i: (i, 0)),
            _const_spec((1, D_MODEL)),
            _const_spec((D_MODEL, N_IN_SLABS * LANES)),
        ],
        out_specs=[
            pl.BlockSpec((3, 2, tm, LANES), lambda i: (0, 0, i, 0)),
            pl.BlockSpec((N_SSD_SLABS, tm, LANES), lambda i: (0, i, 0)),
            pl.BlockSpec((N_HGRN_SLABS, tm, LANES), lambda i: (0, i, 0)),
        ],
        out_shape=[
            jax.ShapeDtypeStruct((3, 2, n, LANES), F32),
            jax.ShapeDtypeStruct((N_SSD_SLABS, n, LANES), F32),
            jax.ShapeDtypeStruct((N_HGRN_SLABS, n, LANES), F32),
        ],
        compiler_params=pltpu.CompilerParams(
            dimension_semantics=("arbitrary",), vmem_limit_bytes=VMEM_LIMIT),
        name="inproj",
    )(h, nw, w)


def _attn_bias():
    qi = np.arange(ATTN_BLK)[:, None]
    ki = np.arange(2 * ATTN_BLK)[None, :]
    band = (ki >= qi) & (ki <= qi + ATTN_BLK)
    planes = [band, band & (ki >= ATTN_BLK)]
    out = np.stack([np.where(np.concatenate([p, p], axis=0), 0.0, MASK_VALUE) for p in planes])
    return out.astype(np.float32)


def _attn_body(in_ref, bias_ref, out_ref, kv_ref, acc_ref, m_ref, l_ref):
    T = ATTN_TILE
    U = ATTN_UNROLL
    j = pl.program_id(1)

    @pl.when(j == 0)
    def _():
        def zero_prev(i, carry):
            rows = pl.ds(pl.multiple_of(i * COPY_ROWS, COPY_ROWS), COPY_ROWS)
            for t in range(2):
                for p in range(2):
                    kv_ref[t, p, rows, :] = jnp.zeros((COPY_ROWS, LANES), F32)
            return carry

        lax.fori_loop(0, T // COPY_ROWS, zero_prev, 0)

    def load_kv(i, carry):
        rows = pl.ds(pl.multiple_of(i * COPY_ROWS, COPY_ROWS), COPY_ROWS)
        dst = pl.ds(pl.multiple_of(T + i * COPY_ROWS, COPY_ROWS), COPY_ROWS)
        for t in range(2):
            for p in range(2):
                kv_ref[t, p, dst, :] = in_ref[t, p, rows, :]
        return carry

    lax.fori_loop(0, T // COPY_ROWS, load_kv, 0)

    h0 = lax.broadcasted_iota(jnp.int32, (1, LANES), 1) < HEAD_DIM
    h0f = h0.astype(F32) * (HEAD_DIM ** -0.5)
    h1f = (HEAD_DIM ** -0.5) - h0f
    first_step = (j == 0).astype(jnp.int32)

    def run_units(pair, units, fresh):
        acc_v, m_v, l_v = acc_ref.at[pair], m_ref.at[pair], l_ref.at[pair]
        work = []
        for q_idx, k_idx, first in units:
            qb = in_ref.at[2, pair][q_idx, :]
            q2 = jnp.concatenate([qb * h0f, qb * h1f], axis=0).astype(BF16)
            kb = kv_ref.at[0, pair][k_idx, :].astype(BF16)
            vb = kv_ref.at[1, pair][k_idx, :].astype(BF16)
            s = _dot_nt(q2, kb) + bias_ref[first]
            work.append(dict(q_idx=q_idx, s=s, vb=vb))
        for d in work:
            s = d["s"]
            m = jnp.max(s, axis=-1, keepdims=True)
            p = jnp.exp(s - m)
            l = jnp.sum(p, axis=-1, keepdims=True)
            o2 = _dot(p.astype(BF16), d["vb"])
            d["o"] = jnp.where(h0, o2[0:ATTN_BLK], o2[ATTN_BLK:])
            d["m"] = jnp.where(h0, m[0:ATTN_BLK], m[ATTN_BLK:])
            d["l"] = jnp.where(h0, l[0:ATTN_BLK], l[ATTN_BLK:])
        for d in work:
            q_idx, o, m, l = d["q_idx"], d["o"], d["m"], d["l"]
            if not fresh:
                m_old = m_v[q_idx, :]
                m_new = jnp.maximum(m_old, m)
                a = jnp.exp(m_old - m_new)
                b = jnp.exp(m - m_new)
                o = a * acc_v[q_idx, :] + b * o
                l = a * l_v[q_idx, :] + b * l
                m = m_new
            acc_v[q_idx, :] = o
            m_v[q_idx, :] = m
            l_v[q_idx, :] = l

    def per_pair(pair, carry0):
        def p1(it, carry):
            units = []
            for u in range(U):
                n = it * U + u
                r0 = pl.multiple_of(n * ATTN_BLK, ATTN_BLK)
                units.append((pl.ds(r0, ATTN_BLK), pl.ds(r0 + T - ATTN_BLK, 2 * ATTN_BLK),
                              first_step * (n == 0).astype(jnp.int32)))
            run_units(pair, units, True)
            return carry

        lax.fori_loop(0, T // ATTN_BLK // U, p1, 0)

        def p2(it, carry):
            units = []
            for u in range(U):
                idx = it * U + u
                c = lax.shift_right_logical(idx, 2)
                r = idx & 3
                units.append((pl.ds(c * 512 + r, ATTN_BLK, stride=4),
                              pl.ds(T + (c - 1) * 512 + r, 2 * ATTN_BLK, stride=4),
                              first_step * (c == 0).astype(jnp.int32)))
            run_units(pair, units, False)
            return carry

        lax.fori_loop(0, (T // 512) * 4 // U, p2, 0)

        def p3(it, carry):
            units = []
            for u in range(U):
                r = it * U + u
                units.append((pl.ds(r, ATTN_BLK, stride=16), pl.ds(r, 2 * ATTN_BLK, stride=16), first_step))
            run_units(pair, units, False)
            return carry

        lax.fori_loop(0, 16 // U, p3, 0)
        return carry0

    lax.fori_loop(0, 2, per_pair, 0)

    def finish(i, carry):
        rows = pl.ds(pl.multiple_of(i * COPY_ROWS, COPY_ROWS), COPY_ROWS)
        src = pl.ds(pl.multiple_of(T + i * COPY_ROWS, COPY_ROWS), COPY_ROWS)
        for p in range(2):
            out_ref[p, rows, :] = acc_ref[p, rows, :] / l_ref[p, rows, :]
            for t in range(2):
                kv_ref[t, p, rows, :] = kv_ref[t, p, src, :]
        return carry

    lax.fori_loop(0, T // COPY_ROWS, finish, 0)


def _attn_call(attn_in, bias, batch, seq):
    T = ATTN_TILE
    nc = seq // T
    n = batch * seq
    return pl.pallas_call(
        _attn_body,
        grid=(batch, nc),
        in_specs=[pl.BlockSpec((3, 2, T, LANES), lambda b, j: (0, 0, b * nc + j, 0)),
                  _const_spec((2, 2 * ATTN_BLK, 2 * ATTN_BLK))],
        out_specs=pl.BlockSpec((2, T, LANES), lambda b, j: (0, b * nc + j, 0)),
        out_shape=jax.ShapeDtypeStruct((2, n, LANES), F32),
        scratch_shapes=[
            pltpu.VMEM((2, 2, 2 * T, LANES), F32),
            pltpu.VMEM((2, T, LANES), F32),
            pltpu.VMEM((2, T, LANES), F32),
            pltpu.VMEM((2, T, LANES), F32),
        ],
        compiler_params=pltpu.CompilerParams(
            dimension_semantics=("arbitrary", "arbitrary"), vmem_limit_bytes=VMEM_LIMIT),
        name="dilated_attn",
    )(attn_in, bias)


def _softplus(x):
    return jnp.maximum(x, 0.0) + _log1p_exp_neg_abs(x)


def _ssd_body(in_ref, cw_ref, cb_ref, dtb_ref, alog_ref, dsk_ref, nw_ref, tri_ref, out_ref,
              xpad_ref, xc_ref, st_ref):
    T = SSM_TILE
    C = SSM_CHUNK
    U = MIXER_UNROLL
    j = pl.program_id(1)

    @pl.when(j == 0)
    def _():
        st_ref[...] = jnp.zeros_like(st_ref)
        xpad_ref[:, 0:8, :] = jnp.zeros((8, 8, LANES), F32)

    @pl.when(j > 0)
    def _():
        xpad_ref[:, 5:8, :] = xpad_ref[:, T + 5:T + 8, :]

    def conv(s, carry):
        xpad_ref[s, 8:T + 8, :] = in_ref[4 + s]
        acc = cb_ref[pl.ds(s, 1), :] + cw_ref[0, pl.ds(s, 1), :] * xpad_ref[s, 5:5 + T, :]
        for t in range(1, SSM_CONV):
            acc = acc + cw_ref[t, pl.ds(s, 1), :] * xpad_ref[s, 5 + t:5 + t + T, :]
        xc_ref[s] = _silu_tanh(acc)
        return carry

    lax.fori_loop(0, 8, conv, 0)

    h0 = lax.broadcasted_iota(jnp.int32, (1, LANES), 1) < HEAD_DIM
    ti = lax.broadcasted_iota(jnp.int32, (C, C), 0)
    si = lax.broadcasted_iota(jnp.int32, (C, C), 1)
    causal = si <= ti
    a_neg = -jnp.exp(alog_ref[...])

    def step(it, carry):
        chunks = []
        for u in range(U):
            chunks.append(dict(rows=pl.ds(pl.multiple_of((it * U + u) * C, C), C)))
        for d in chunks:
            rows = d["rows"]
            dt = _softplus(in_ref[12, rows, :] + dtb_ref[...])
            acs = _exact_dot01(tri_ref[...], dt * a_neg)
            d["acs"] = acs
            d["acs_t"] = acs.T
            d["ldt_t"] = jnp.log(dt).T
        for d in chunks:
            rows = d["rows"]
            d["bm_t"], d["cmb"], d["cb"] = [], [], []
            for g in range(SSM_GROUPS):
                bm_t = xc_ref[4 + g, rows, :].T
                cmb = xc_ref[6 + g, rows, :].astype(BF16)
                d["bm_t"].append(bm_t)
                d["cmb"].append(cmb)
                d["cb"].append(_dot(cmb, bm_t.astype(BF16)))
        for d in chunks:
            rows = d["rows"]
            d["xs"] = [xc_ref[sl, rows, :] for sl in range(4)]
            xb = [x.astype(BF16) for x in d["xs"]]
            d["yd"], d["stn"], d["ecol"], d["cdec"] = [], [], [], []
            for h in range(SSM_HEADS):
                g, sl = h // 4, h // 2
                arow = d["acs_t"][h:h + 1, :] - d["ldt_t"][h:h + 1, :]
                acol = jnp.broadcast_to(d["acs"][:, h:h + 1], (C, C))
                alast = d["acs_t"][h:h + 1, C - 1:C]
                lmat = jnp.where(causal, jnp.exp(jnp.where(causal, acol - arow, 0.0)), 0.0)
                d["yd"].append(_dot((d["cb"][g] * lmat).astype(BF16), xb[sl]))
                d["stn"].append(_dot((d["bm_t"][g] * jnp.exp(alast - arow)).astype(BF16), xb[sl]))
                d["ecol"].append(jnp.exp(acol))
                d["cdec"].append(jnp.exp(alast))
        for sl in range(4):
            st = st_ref[sl]
            for d in chunks:
                ev, od = 2 * sl, 2 * sl + 1
                y_off = _dot(d["cmb"][sl // 2], st.astype(BF16)) * jnp.where(h0, d["ecol"][ev], d["ecol"][od])
                st = (st * jnp.where(h0, d["cdec"][ev], d["cdec"][od])
                      + jnp.where(h0, d["stn"][ev], d["stn"][od]))
                y = jnp.where(h0, d["yd"][ev], d["yd"][od]) + y_off + dsk_ref[sl:sl + 1, :] * d["xs"][sl]
                d.setdefault("ys", []).append(y * _silu_tanh(in_ref[sl, d["rows"], :]))
            st_ref[sl] = st
        for d in chunks:
            rows = d["rows"]
            for g in range(SSM_GROUPS):
                ya, yb = d["ys"][2 * g], d["ys"][2 * g + 1]
                ss = jnp.sum(ya * ya, axis=-1, keepdims=True) + jnp.sum(yb * yb, axis=-1, keepdims=True)
                r = lax.rsqrt(ss * (1.0 / (2 * LANES)) + NORM_EPS)
                out_ref[2 * g, rows, :] = ya * r * nw_ref[2 * g:2 * g + 1, :]
                out_ref[2 * g + 1, rows, :] = yb * r * nw_ref[2 * g + 1:2 * g + 2, :]
        return carry

    lax.fori_loop(0, T // (C * U), step, 0)


def _ssd_call(ssd_in, cw, cb, dtb, alog, dsk, nw, tri, batch, seq):
    T = SSM_TILE
    nt = seq // T
    n = batch * seq
    return pl.pallas_call(
        _ssd_body,
        grid=(batch, nt),
        in_specs=[
            pl.BlockSpec((N_SSD_SLABS, T, LANES), lambda b, j: (0, b * nt + j, 0)),
            _const_spec((SSM_CONV, 8, LANES)),
            _const_spec((8, LANES)),
            _const_spec((1, LANES)),
            _const_spec((1, LANES)),
            _const_spec((4, LANES)),
            _const_spec((4, LANES)),
            _const_spec((SSM_CHUNK, SSM_CHUNK)),
        ],
        out_specs=pl.BlockSpec((4, T, LANES), lambda b, j: (0, b * nt + j, 0)),
        out_shape=jax.ShapeDtypeStruct((4, n, LANES), F32),
        scratch_shapes=[
            pltpu.VMEM((8, T + 8, LANES), F32),
            pltpu.VMEM((8, T, LANES), F32),
            pltpu.VMEM((4, SSM_STATE, LANES), F32),
        ],
        compiler_params=pltpu.CompilerParams(
            dimension_semantics=("arbitrary", "arbitrary"), vmem_limit_bytes=VMEM_LIMIT),
        name="ssd",
    )(ssd_in, cw, cb, dtb, alog, dsk, nw, tri)


def _hgrn_level_consts():
    C = HGRN_CHUNK
    t = np.arange(C)[:, None]
    lane = np.arange(LANES)[None, :]
    s = lane % HEAD_DIM
    out = np.zeros((len(HGRN_LEVELS), 5, C, LANES), np.float32)
    for i, m in enumerate(HGRN_LEVELS):
        second = ((t // m) % 2) == 1
        out[i, 0] = np.where(second, 1.0, -1.0)
        out[i, 1] = second
        out[i, 2] = (~second) & (lane < HEAD_DIM)
        out[i, 3] = (~second) & (lane >= HEAD_DIM)
        out[i, 4] = (t // (2 * m)) == (s // (2 * m))
    return out


def _hgrn_body(in_ref, loglb_ref, log1mlb_ref, nw_ref, tri_ref, lv_ref, bd_ref, out_ref,
               st_ref, b_ref):
    T = HGRN_TILE
    C = HGRN_CHUNK
    U = HGRN_UNROLL
    B0 = 8
    j = pl.program_id(1)

    @pl.when(j == 0)
    def _():
        st_ref[...] = jnp.zeros_like(st_ref)

    b_ref[:, 0:B0, :] = jnp.zeros((2 * U, B0, LANES), F32)

    lane = lax.broadcasted_iota(jnp.int32, (1, LANES), 1)
    h0 = lane < HEAD_DIM
    h0f = h0.astype(F32)
    h1f = 1.0 - h0f
    t_col = lax.broadcasted_iota(jnp.int32, (C, 1), 0)
    odd = (t_col & 1) == 1
    low4 = (t_col & 4) == 0
    bd = bd_ref[...]
    bd_mask = bd > 0

    def bcast_rows(bv, row_of_group):
        return jnp.concatenate(
            [jnp.broadcast_to(bv[pl.ds(B0 + row_of_group(g), 1), :], (8, LANES)) for g in range(C // 8)],
            axis=0)

    def ref_rows(bv, b, m):
        if m >= 4:
            return bcast_rows(bv, lambda g: (8 * g // (2 * m)) * 2 * m + m - 1)
        if m == 2:
            return jnp.where(low4, bcast_rows(bv, lambda g: 8 * g + 1), bcast_rows(bv, lambda g: 8 * g + 5))
        return jnp.where(odd, bv[pl.ds(B0 - 1, C), :], b)

    def step(it, carry):
        units = []
        for u in range(U):
            rows = pl.ds(pl.multiple_of((it * U + u) * C, C), C)
            for sp in range(2):
                units.append(dict(sp=sp, rows=rows, slot=2 * u + sp))
        for d in units:
            sp, rows = d["sp"], d["rows"]
            hf = in_ref[2 + sp, rows, :]
            d["v"] = in_ref[4 + sp, rows, :]
            d["q"] = _silu_tanh(in_ref[sp, rows, :])
            l1p = _log1p_exp_neg_abs(hf)
            log_sig = jnp.minimum(hf, 0.0) - l1p
            a1 = loglb_ref[sp:sp + 1, :]
            a2 = log1mlb_ref[sp:sp + 1, :] + log_sig
            lf = jnp.maximum(a1, a2) + _log1p_exp_neg_abs(a1 - a2)
            d["kk"] = jnp.exp(log1mlb_ref[sp:sp + 1, :] - jnp.maximum(hf, 0.0) - l1p)
            d["b"] = _exact_dot01(tri_ref[...], lf)
            b_ref[d["slot"], B0:B0 + C, :] = d["b"]
        for d in units:
            b = d["b"]
            d["blast"] = b[C - 1:C, :]
            d["qe"] = (d["q"] * jnp.exp(b)).astype(BF16)
            kd = (d["kk"] * jnp.exp(d["blast"] - b)).astype(BF16)
            d["upd"] = _dot_tn(d["v"].astype(BF16), kd)
            d["att"] = None
        for sp in range(2):
            st = st_ref[sp]
            for d in units:
                if d["sp"] == sp:
                    d["o"] = _dot_nt(d["qe"], st.astype(BF16))
                    st = st * jnp.exp(d["blast"]) + jnp.where(bd_mask, d["upd"], 0.0)
            st_ref[sp] = st
        for d in units:
            d["o"] = d["o"] + _dot((d["q"] * d["kk"]).astype(BF16), bd) * d["v"]
        for i, m in enumerate(HGRN_LEVELS):
            for d in units:
                rm = ref_rows(b_ref.at[d["slot"]], d["b"], m)
                em = jnp.exp((d["b"] - rm) * lv_ref[i, 0])
                qs = (d["q"] * em * lv_ref[i, 1]).astype(BF16)
                ke = d["kk"] * em
                ks2 = jnp.concatenate([ke * lv_ref[i, 2], ke * lv_ref[i, 3]], axis=0).astype(BF16)
                p = _dot_nt(qs, ks2) * lv_ref[i, 4]
                d["att"] = p if d["att"] is None else d["att"] + p
        for d in units:
            v = d["v"]
            vbd = jnp.concatenate([v * h0f, v * h1f], axis=0).astype(BF16)
            d["o"] = d["o"] + _dot(d["att"].astype(BF16), vbd)
        for d in units:
            sp, rows, o = d["sp"], d["rows"], d["o"]
            o2 = o * o
            ss0 = jnp.sum(jnp.where(h0, o2, 0.0), axis=-1, keepdims=True)
            ss1 = jnp.sum(jnp.where(h0, 0.0, o2), axis=-1, keepdims=True)
            r = lax.rsqrt(jnp.where(h0, ss0, ss1) * (1.0 / HEAD_DIM) + NORM_EPS)
            out_ref[sp, rows, :] = o * r * nw_ref[sp:sp + 1, :] * _silu_tanh(in_ref[6 + sp, rows, :])
        return carry

    lax.fori_loop(0, T // (C * U), step, 0)


def _hgrn_call(hgrn_in, loglb, log1mlb, nw, tri, lv, bd, batch, seq):
    T = HGRN_TILE
    nt = seq // T
    n = batch * seq
    assert HGRN_CHUNK == HEAD_DIM
    return pl.pallas_call(
        _hgrn_body,
        grid=(batch, nt),
        in_specs=[
            pl.BlockSpec((N_HGRN_SLABS, T, LANES), lambda b, j: (0, b * nt + j, 0)),
            _const_spec((2, LANES)),
            _const_spec((2, LANES)),
            _const_spec((2, LANES)),
            _const_spec((HGRN_CHUNK, HGRN_CHUNK)),
            _const_spec((len(HGRN_LEVELS), 5, HGRN_CHUNK, LANES)),
            _const_spec((LANES, LANES)),
        ],
        out_specs=pl.BlockSpec((2, T, LANES), lambda b, j: (0, b * nt + j, 0)),
        out_shape=jax.ShapeDtypeStruct((2, n, LANES), F32),
        scratch_shapes=[
            pltpu.VMEM((2, LANES, LANES), F32),
            pltpu.VMEM((2 * HGRN_UNROLL, 8 + HGRN_CHUNK, LANES), F32),
        ],
        compiler_params=pltpu.CompilerParams(
            dimension_semantics=("arbitrary", "arbitrary"), vmem_limit_bytes=VMEM_LIMIT),
        name="hgrn2",
    )(hgrn_in, loglb, log1mlb, nw, tri, lv, bd)


def _outproj_body(h_ref, ya_ref, ys_ref, yh_ref, w_ref, o_ref):
    parts = [ya_ref[0], ya_ref[1], ys_ref[0], ys_ref[1], ys_ref[2], ys_ref[3], yh_ref[0], yh_ref[1]]
    y = jnp.concatenate([p.astype(BF16) for p in parts], axis=1)
    o_ref[...] = h_ref[...] + _dot(y, w_ref[...])


def _outproj_call(h, ya, ys, yh, w):
    n = h.shape[0]
    tm = TOKEN_TILE
    return pl.pallas_call(
        _outproj_body,
        grid=(n // tm,),
        in_specs=[
            pl.BlockSpec((tm, D_MODEL), lambda i: (i, 0)),
            pl.BlockSpec((2, tm, LANES), lambda i: (0, i, 0)),
            pl.BlockSpec((4, tm, LANES), lambda i: (0, i, 0)),
            pl.BlockSpec((2, tm, LANES), lambda i: (0, i, 0)),
            _const_spec((D_MODEL, D_MODEL)),
        ],
        out_specs=pl.BlockSpec((tm, D_MODEL), lambda i: (i, 0)),
        out_shape=jax.ShapeDtypeStruct((n, D_MODEL), F32),
        compiler_params=pltpu.CompilerParams(
            dimension_semantics=("arbitrary",), vmem_limit_bytes=VMEM_LIMIT),
        name="outproj",
    )(h, ya, ys, yh, w)


def _permute_w_in(w):
    o = 0
    cols = {}
    for name, size in (("aq", 256), ("ak", 256), ("av", 256), ("z", 512), ("x", 512), ("B", 256),
                       ("C", 256), ("dt", 8), ("hq", 256), ("hf", 256), ("hi", 256), ("hg", 256)):
        cols[name] = w[:, o:o + size]
        o += size
    dt = jnp.pad(cols["dt"], ((0, 0), (0, LANES - SSM_HEADS)))
    order = ["ak", "av", "aq", "z", "x", "B", "C"]
    return jnp.concatenate([cols[k] for k in order] + [dt] + [cols[k] for k in ("hq", "hf", "hi", "hg")],
                           axis=1)


def _pad_lanes(v):
    return jnp.pad(v, (0, LANES - v.shape[0])).reshape(1, LANES)


def kernel(x, ffn1_norm, ffn1_w_gate, ffn1_w_up, ffn1_w_down, mix_norm, w_in, conv_w, conv_b, dt_bias, a_log, d_skip, ssm_norm, hgrn_lb_logits, hgrn_norm, w_out, ffn2_norm, ffn2_w_gate, ffn2_w_up, ffn2_w_down, final_norm):
    batch, seq, d = x.shape
    depth = w_in.shape[0]
    assert d == D_MODEL and seq % ATTN_TILE == 0
    h = x.reshape(batch * seq, d)

    sm = jax.nn.softmax(hgrn_lb_logits.astype(F32), axis=0)
    lower = jnp.clip(jnp.cumsum(sm, axis=0) - sm[0], HGRN_LB_FLOOR, 1.0 - 1e-6)
    tri = jnp.asarray(np.tril(np.ones((SSM_CHUNK, SSM_CHUNK), np.float32)), BF16)
    tri_h = jnp.asarray(np.tril(np.ones((HGRN_CHUNK, HGRN_CHUNK), np.float32)), BF16)
    lv = jnp.asarray(_hgrn_level_consts())
    bd = jnp.asarray(np.kron(np.eye(2, dtype=np.float32), np.ones((HEAD_DIM, HEAD_DIM), np.float32)), BF16)
    fw = final_norm.reshape(1, d)
    attn_bias = jnp.asarray(_attn_bias())

    for layer in range(depth):
        h = _ffn_call(h, ffn1_norm[layer].reshape(1, d), ffn1_w_gate[layer].astype(BF16),
                      ffn1_w_up[layer].astype(BF16), ffn1_w_down[layer].astype(BF16), fw, False)
        attn_in, ssd_in, hgrn_in = _inproj_call(h, mix_norm[layer].reshape(1, d),
                                                _permute_w_in(w_in[layer]).astype(BF16))
        y_attn = _attn_call(attn_in, attn_bias, batch, seq)
        y_ssm = _ssd_call(
            ssd_in,
            conv_w[layer].reshape(SSM_CONV, 8, LANES), conv_b[layer].reshape(8, LANES),
            _pad_lanes(dt_bias[layer]), _pad_lanes(a_log[layer]),
            jnp.repeat(d_skip[layer], HEAD_DIM).reshape(4, LANES), ssm_norm[layer].reshape(4, LANES),
            tri, batch, seq)
        lb = lower[layer]
        y_hgrn = _hgrn_call(hgrn_in, jnp.log(lb).reshape(2, LANES), jnp.log1p(-lb).reshape(2, LANES),
                            hgrn_norm[layer].reshape(2, LANES), tri_h, lv, bd, batch, seq)
        h = _outproj_call(h, y_attn, y_ssm, y_hgrn, w_out[layer].astype(BF16))
        h = _ffn_call(h, ffn2_norm[layer].reshape(1, d), ffn2_w_gate[layer].astype(BF16),
                      ffn2_w_up[layer].astype(BF16), ffn2_w_down[layer].astype(BF16), fw,
                      layer == depth - 1)
    return h.reshape(batch, seq, d)
```

```python
import functools

import numpy as np
import jax
import jax.numpy as jnp
from jax import lax
from jax.experimental import pallas as pl
from jax.experimental.pallas import tpu as pltpu

F32 = jnp.float32
BF16 = jnp.bfloat16

D_MODEL = 1024
HEAD_DIM = 64
LANES = 128
ATTN_WIDTH = 256
DILATED_PATTERNS = ((128, 1), (512, 4), (2048, 16))
ATTN_BLK = 128
ATTN_TILE = 2048
MASK_VALUE = -1e30
SSM_WIDTH = 512
SSM_HEADS = 8
SSM_GROUPS = 2
SSM_STATE = 128
SSM_CONV = 4
SSM_CHUNK = 128
SSM_BC = SSM_GROUPS * SSM_STATE
SSM_TILE = 1024
HGRN_WIDTH = 256
HGRN_KEY_WIDTH = 256
HGRN_CHUNK = 64
HGRN_LEVELS = (32, 16, 8, 4, 2, 1)
HGRN_TILE = 1024
HGRN_LB_FLOOR = 1e-20
D_FF = 2816
NORM_EPS = 1e-6
TOKEN_TILE = 512
COPY_ROWS = 256
ATTN_UNROLL = 4
MIXER_UNROLL = 2
HGRN_UNROLL = 8
FF_CHUNK = 256
VMEM_LIMIT = 56 * 1024 * 1024

N_ATTN_SLABS = 6
N_SSD_SLABS = 13
N_HGRN_SLABS = 8
N_IN_SLABS = N_ATTN_SLABS + N_SSD_SLABS + N_HGRN_SLABS


def _silu(x):
    return x * jax.nn.sigmoid(x)


def _silu_tanh(x):
    return x * (0.5 + 0.5 * jnp.tanh(0.5 * x))


def _log1p_exp_neg_abs(x):
    return jnp.log(1.0 + jnp.exp(-jnp.abs(x)))


def _rms_scale(x, w):
    return x * lax.rsqrt(jnp.mean(x * x, axis=-1, keepdims=True) + NORM_EPS) * w


def _dot(a, b):
    return jnp.dot(a, b, preferred_element_type=F32)


def _dot_nt(a, b):
    return lax.dot_general(a, b, (((1,), (1,)), ((), ())), preferred_element_type=F32)


def _dot_tn(a, b):
    return lax.dot_general(a, b, (((0,), (0,)), ((), ())), preferred_element_type=F32)


def _exact_dot01(g, x):
    x1 = x.astype(BF16)
    r1 = x - x1.astype(F32)
    x2 = r1.astype(BF16)
    x3 = (r1 - x2.astype(F32)).astype(BF16)
    y = _dot(g, jnp.concatenate([x1, x2, x3], axis=1))
    return y[:, 0:LANES] + y[:, LANES:2 * LANES] + y[:, 2 * LANES:3 * LANES]


def _const_spec(shape):
    nd = len(shape)
    return pl.BlockSpec(shape, lambda *_: (0,) * nd, pipeline_mode=pl.Buffered(1))


def _ffn_tail(x, nw_ref, wg_ref, wu_ref, wd_ref, fw_ref, o_ref, a_ref, final):
    xn = _rms_scale(x, nw_ref[...]).astype(BF16)
    for c in range(D_FF // FF_CHUNK):
        sl = slice(c * FF_CHUNK, (c + 1) * FF_CHUNK)
        g = _dot(xn, wg_ref[:, sl])
        u = _dot(xn, wu_ref[:, sl])
        a_ref[:, sl] = (_silu(g) * u).astype(BF16)
    y = x + 0.5 * _dot(a_ref[...], wd_ref[...])
    if final:
        y = _rms_scale(y, fw_ref[...])
    o_ref[...] = y


def _ffn_body(x_ref, nw_ref, wg_ref, wu_ref, wd_ref, fw_ref, o_ref, a_ref, *, final):
    _ffn_tail(x_ref[...], nw_ref, wg_ref, wu_ref, wd_ref, fw_ref, o_ref, a_ref, final)


def _ffn_call(h, nw, wg, wu, wd, fw, final):
    n = h.shape[0]
    tm = TOKEN_TILE
    return pl.pallas_call(
        functools.partial(_ffn_body, final=final),
        grid=(n // tm,),
        in_specs=[
            pl.BlockSpec((tm, D_MODEL), lambda i: (i, 0)),
            _const_spec((1, D_MODEL)),
            _const_spec((D_MODEL, D_FF)),
            _const_spec((D_MODEL, D_FF)),
            _const_spec((D_FF, D_MODEL)),
            _const_spec((1, D_MODEL)),
        ],
        out_specs=pl.BlockSpec((tm, D_MODEL), lambda i: (i, 0)),
        out_shape=jax.ShapeDtypeStruct((n, D_MODEL), F32),
        scratch_shapes=[pltpu.VMEM((tm, D_FF), BF16)],
        compiler_params=pltpu.CompilerParams(
            dimension_semantics=("arbitrary",), vmem_limit_bytes=VMEM_LIMIT),
        name="ffn",
    )(h, nw, wg, wu, wd, fw)


def _inproj_body(x_ref, nw_ref, w_ref, attn_ref, ssd_ref, hgrn_ref):
    xn = _rms_scale(x_ref[...], nw_ref[...]).astype(BF16)

    def put(slab, val):
        if slab < N_ATTN_SLABS:
            attn_ref[slab // 2, slab % 2] = val
        elif slab < N_ATTN_SLABS + N_SSD_SLABS:
            ssd_ref[slab - N_ATTN_SLABS] = val
        else:
            hgrn_ref[slab - N_ATTN_SLABS - N_SSD_SLABS] = val

    for s0 in range(0, N_IN_SLABS, 2):
        s1 = min(s0 + 2, N_IN_SLABS)
        res = _dot(xn, w_ref[:, s0 * LANES:s1 * LANES])
        for s in range(s0, s1):
            put(s, res[:, (s - s0) * LANES:(s - s0 + 1) * LANES])


def _inproj_call(h, nw, w):
    n = h.shape[0]
    tm = TOKEN_TILE
    return pl.pallas_call(
        _inproj_body,
        grid=(n // tm,),
        in_specs=[
            pl.BlockSpec((tm, D_MODEL), lambda i: (i, 0)),
            _const_spec((1, D_MODEL)),
            _const_spec((D_MODEL, N_IN_SLABS * LANES)),
        ],
        out_specs=[
            pl.BlockSpec((3, 2, tm, LANES), lambda i: (0, 0, i, 0)),
            pl.BlockSpec((N_SSD_SLABS, tm, LANES), lambda i: (0, i, 0)),
            pl.BlockSpec((N_HGRN_SLABS, tm, LANES), lambda i: (0, i, 0)),
        ],
        out_shape=[
            jax.ShapeDtypeStruct((3, 2, n, LANES), F32),
            jax.ShapeDtypeStruct((N_SSD_SLABS, n, LANES), F32),
            jax.ShapeDtypeStruct((N_HGRN_SLABS, n, LANES), F32),
        ],
        compiler_params=pltpu.CompilerParams(
            dimension_semantics=("arbitrary",), vmem_limit_bytes=VMEM_LIMIT),
        name="inproj",
    )(h, nw, w)


def _attn_bias():
    qi = np.arange(ATTN_BLK)[:, None]
    ki = np.arange(2 * ATTN_BLK)[None, :]
    band = (ki >= qi) & (ki <= qi + ATTN_BLK)
    planes = [band, band & (ki >= ATTN_BLK)]
    out = np.stack([np.where(np.concatenate([p, p], axis=0), 0.0, MASK_VALUE) for p in planes])
    return out.astype(np.float32)


def _attn_body(in_ref, bias_ref, out_ref, kv_ref, acc_ref, m_ref, l_ref):
    T = ATTN_TILE
    U = ATTN_UNROLL
    j = pl.program_id(1)

    @pl.when(j == 0)
    def _():
        def zero_prev(i, carry):
            rows = pl.ds(pl.multiple_of(i * COPY_ROWS, COPY_ROWS), COPY_ROWS)
            for t in range(2):
                for p in range(2):
                    kv_ref[t, p, rows, :] = jnp.zeros((COPY_ROWS, LANES), F32)
            return carry

        lax.fori_loop(0, T // COPY_ROWS, zero_prev, 0)

    def load_kv(i, carry):
        rows = pl.ds(pl.multiple_of(i * COPY_ROWS, COPY_ROWS), COPY_ROWS)
        dst = pl.ds(pl.multiple_of(T + i * COPY_ROWS, COPY_ROWS), COPY_ROWS)
        for t in range(2):
            for p in range(2):
                kv_ref[t, p, dst, :] = in_ref[t, p, rows, :]
        return carry

    lax.fori_loop(0, T // COPY_ROWS, load_kv, 0)

    h0 = lax.broadcasted_iota(jnp.int32, (1, LANES), 1) < HEAD_DIM
    h0f = h0.astype(F32) * (HEAD_DIM ** -0.5)
    h1f = (HEAD_DIM ** -0.5) - h0f
    first_step = (j == 0).astype(jnp.int32)

    def run_units(pair, units, fresh):
        acc_v, m_v, l_v = acc_ref.at[pair], m_ref.at[pair], l_ref.at[pair]
        work = []
        for q_idx, k_idx, first in units:
            qb = in_ref.at[2, pair][q_idx, :]
            q2 = jnp.concatenate([qb * h0f, qb * h1f], axis=0).astype(BF16)
            kb = kv_ref.at[0, pair][k_idx, :].astype(BF16)
            vb = kv_ref.at[1, pair][k_idx, :].astype(BF16)
            s = _dot_nt(q2, kb) + bias_ref[first]
            work.append(dict(q_idx=q_idx, s=s, vb=vb))
        for d in work:
            s = d["s"]
            m = jnp.max(s, axis=-1, keepdims=True)
            p = jnp.exp(s - m)
            l = jnp.sum(p, axis=-1, keepdims=True)
            o2 = _dot(p.astype(BF16), d["vb"])
            d["o"] = jnp.where(h0, o2[0:ATTN_BLK], o2[ATTN_BLK:])
            d["m"] = jnp.where(h0, m[0:ATTN_BLK], m[ATTN_BLK:])
            d["l"] = jnp.where(h0, l[0:ATTN_BLK], l[ATTN_BLK:])
        for d in work:
            q_idx, o, m, l = d["q_idx"], d["o"], d["m"], d["l"]
            if not fresh:
                m_old = m_v[q_idx, :]
                m_new = jnp.maximum(m_old, m)
                a = jnp.exp(m_old - m_new)
                b = jnp.exp(m - m_new)
                o = a * acc_v[q_idx, :] + b * o
                l = a * l_v[q_idx, :] + b * l
                m = m_new
            acc_v[q_idx, :] = o
            m_v[q_idx, :] = m
            l_v[q_idx, :] = l

    def per_pair(pair, carry0):
        def p1(it, carry):
            units = []
            for u in range(U):
                n = it * U + u
                r0 = pl.multiple_of(n * ATTN_BLK, ATTN_BLK)
                units.append((pl.ds(r0, ATTN_BLK), pl.ds(r0 + T - ATTN_BLK, 2 * ATTN_BLK),
                              jnp.where(n == 0, first_step, 0)))
            run_units(pair, units, True)
            return carry

        lax.fori_loop(0, T // ATTN_BLK // U, p1, 0)

        def p2(it, carry):
            units = []
            for u in range(U):
                idx = it * U + u
                c = lax.shift_right_logical(idx, 2)
                r = idx & 3
                units.append((pl.ds(c * 512 + r, ATTN_BLK, stride=4),
                              pl.ds(T + (c - 1) * 512 + r, 2 * ATTN_BLK, stride=4),
                              jnp.where(c == 0, first_step, 0)))
            run_units(pair, units, False)
            return carry

        lax.fori_loop(0, (T // 512) * 4 // U, p2, 0)

        def p3(it, carry):
            units = []
            for u in range(U):
                r = it * U + u
                units.append((pl.ds(r, ATTN_BLK, stride=16), pl.ds(r, 2 * ATTN_BLK, stride=16), first_step))
            run_units(pair, units, False)
            return carry

        lax.fori_loop(0, 16 // U, p3, 0)
        return carry0

    lax.fori_loop(0, 2, per_pair, 0)

    def finish(i, carry):
        rows = pl.ds(pl.multiple_of(i * COPY_ROWS, COPY_ROWS), COPY_ROWS)
        src = pl.ds(pl.multiple_of(T + i * COPY_ROWS, COPY_ROWS), COPY_ROWS)
        for p in range(2):
            out_ref[p, rows, :] = acc_ref[p, rows, :] / l_ref[p, rows, :]
            for t in range(2):
                kv_ref[t, p, rows, :] = kv_ref[t, p, src, :]
        return carry

    lax.fori_loop(0, T // COPY_ROWS, finish, 0)


def _attn_call(attn_in, bias, batch, seq):
    T = ATTN_TILE
    nc = seq // T
    n = batch * seq
    return pl.pallas_call(
        _attn_body,
        grid=(batch, nc),
        in_specs=[pl.BlockSpec((3, 2, T, LANES), lambda b, j: (0, 0, b * nc + j, 0)),
                  _const_spec((2, 2 * ATTN_BLK, 2 * ATTN_BLK))],
        out_specs=pl.BlockSpec((2, T, LANES), lambda b, j: (0, b * nc + j, 0)),
        out_shape=jax.ShapeDtypeStruct((2, n, LANES), F32),
        scratch_shapes=[
            pltpu.VMEM((2, 2, 2 * T, LANES), F32),
            pltpu.VMEM((2, T, LANES), F32),
            pltpu.VMEM((2, T, LANES), F32),
            pltpu.VMEM((2, T, LANES), F32),
        ],
        compiler_params=pltpu.CompilerParams(
            dimension_semantics=("arbitrary", "arbitrary"), vmem_limit_bytes=VMEM_LIMIT),
        name="dilated_attn",
    )(attn_in, bias)


def _softplus(x):
    return jnp.maximum(x, 0.0) + _log1p_exp_neg_abs(x)


def _ssd_body(in_ref, cw_ref, cb_ref, dtb_ref, alog_ref, dsk_ref, nw_ref, tri_ref, out_ref,
              xpad_ref, xc_ref, st_ref):
    T = SSM_TILE
    C = SSM_CHUNK
    U = MIXER_UNROLL
    j = pl.program_id(1)

    @pl.when(j == 0)
    def _():
        st_ref[...] = jnp.zeros_like(st_ref)
        xpad_ref[:, 0:8, :] = jnp.zeros((8, 8, LANES), F32)

    @pl.when(j > 0)
    def _():
        xpad_ref[:, 5:8, :] = xpad_ref[:, T + 5:T + 8, :]

    def conv(s, carry):
        xpad_ref[s, 8:T + 8, :] = in_ref[4 + s]
        acc = cb_ref[pl.ds(s, 1), :] + cw_ref[0, pl.ds(s, 1), :] * xpad_ref[s, 5:5 + T, :]
        for t in range(1, SSM_CONV):
            acc = acc + cw_ref[t, pl.ds(s, 1), :] * xpad_ref[s, 5 + t:5 + t + T, :]
        xc_ref[s] = _silu_tanh(acc)
        return carry

    lax.fori_loop(0, 8, conv, 0)

    h0 = lax.broadcasted_iota(jnp.int32, (1, LANES), 1) < HEAD_DIM
    ti = lax.broadcasted_iota(jnp.int32, (C, C), 0)
    si = lax.broadcasted_iota(jnp.int32, (C, C), 1)
    causal = si <= ti
    a_neg = -jnp.exp(alog_ref[...])

    def step(it, carry):
        chunks = []
        for u in range(U):
            chunks.append(dict(rows=pl.ds(pl.multiple_of((it * U + u) * C, C), C)))
        for d in chunks:
            rows = d["rows"]
            dt = _softplus(in_ref[12, rows, :] + dtb_ref[...])
            acs = _exact_dot01(tri_ref[...], dt * a_neg)
            d["acs"] = acs
            d["acs_t"] = acs.T
            d["ldt_t"] = jnp.log(dt).T
        for d in chunks:
            rows = d["rows"]
            d["bm_t"], d["cmb"], d["cb"] = [], [], []
            for g in range(SSM_GROUPS):
                bm_t = xc_ref[4 + g, rows, :].T
                cmb = xc_ref[6 + g, rows, :].astype(BF16)
                d["bm_t"].append(bm_t)
                d["cmb"].append(cmb)
                d["cb"].append(_dot(cmb, bm_t.astype(BF16)))
        for d in chunks:
            rows = d["rows"]
            d["xs"] = [xc_ref[sl, rows, :] for sl in range(4)]
            xb = [x.astype(BF16) for x in d["xs"]]
            d["yd"], d["stn"], d["ecol"], d["cdec"] = [], [], [], []
            for h in range(SSM_HEADS):
                g, sl = h // 4, h // 2
                arow = d["acs_t"][h:h + 1, :] - d["ldt_t"][h:h + 1, :]
                acol = jnp.broadcast_to(d["acs"][:, h:h + 1], (C, C))
                alast = d["acs_t"][h:h + 1, C - 1:C]
                lmat = jnp.where(causal, jnp.exp(jnp.where(causal, acol - arow, 0.0)), 0.0)
                d["yd"].append(_dot((d["cb"][g] * lmat).astype(BF16), xb[sl]))
                d["stn"].append(_dot((d["bm_t"][g] * jnp.exp(alast - arow)).astype(BF16), xb[sl]))
                d["ecol"].append(jnp.exp(acol))
                d["cdec"].append(jnp.exp(alast))
        for sl in range(4):
            st = st_ref[sl]
            for d in chunks:
                ev, od = 2 * sl, 2 * sl + 1
                y_off = _dot(d["cmb"][sl // 2], st.astype(BF16)) * jnp.where(h0, d["ecol"][ev], d["ecol"][od])
                st = (st * jnp.where(h0, d["cdec"][ev], d["cdec"][od])
                      + jnp.where(h0, d["stn"][ev], d["stn"][od]))
                y = jnp.where(h0, d["yd"][ev], d["yd"][od]) + y_off + dsk_ref[sl:sl + 1, :] * d["xs"][sl]
                d.setdefault("ys", []).append(y * _silu_tanh(in_ref[sl, d["rows"], :]))
            st_ref[sl] = st
        for d in chunks:
            rows = d["rows"]
            for g in range(SSM_GROUPS):
                ya, yb = d["ys"][2 * g], d["ys"][2 * g + 1]
                ss = jnp.sum(ya * ya, axis=-1, keepdims=True) + jnp.sum(yb * yb, axis=-1, keepdims=True)
                r = lax.rsqrt(ss * (1.0 / (2 * LANES)) + NORM_EPS)
                out_ref[2 * g, rows, :] = ya * r * nw_ref[2 * g:2 * g + 1, :]
                out_ref[2 * g + 1, rows, :] = yb * r * nw_ref[2 * g + 1:2 * g + 2, :]
        return carry

    lax.fori_loop(0, T // (C * U), step, 0)


def _ssd_call(ssd_in, cw, cb, dtb, alog, dsk, nw, tri, batch, seq):
    T = SSM_TILE
    nt = seq // T
    n = batch * seq
    return pl.pallas_call(
        _ssd_body,
        grid=(batch, nt),
        in_specs=[
            pl.BlockSpec((N_SSD_SLABS, T, LANES), lambda b, j: (0, b * nt + j, 0)),
            _const_spec((SSM_CONV, 8, LANES)),
            _const_spec((8, LANES)),
            _const_spec((1, LANES)),
            _const_spec((1, LANES)),
            _const_spec((4, LANES)),
            _const_spec((4, LANES)),
            _const_spec((SSM_CHUNK, SSM_CHUNK)),
        ],
        out_specs=pl.BlockSpec((4, T, LANES), lambda b, j: (0, b * nt + j, 0)),
        out_shape=jax.ShapeDtypeStruct((4, n, LANES), F32),
        scratch_shapes=[
            pltpu.VMEM((8, T + 8, LANES), F32),
            pltpu.VMEM((8, T, LANES), F32),
            pltpu.VMEM((4, SSM_STATE, LANES), F32),
        ],
        compiler_params=pltpu.CompilerParams(
            dimension_semantics=("arbitrary", "arbitrary"), vmem_limit_bytes=VMEM_LIMIT),
        name="ssd",
    )(ssd_in, cw, cb, dtb, alog, dsk, nw, tri)


def _hgrn_level_consts():
    C = HGRN_CHUNK
    t = np.arange(C)[:, None]
    s = np.arange(LANES)[None, :] % HEAD_DIM
    out = np.zeros((len(HGRN_LEVELS), 2, C, LANES), np.float32)
    for i, m in enumerate(HGRN_LEVELS):
        second = ((t // m) % 2) == 1
        out[i, 0] = np.where(second, 1.0, -1.0)
        out[i, 1] = (t // (2 * m) == s // (2 * m)) & second & (((s // m) % 2) == 0)
    return out


def _hgrn_body(in_ref, loglb_ref, log1mlb_ref, nw_ref, tri_ref, lv_ref, bd_ref, out_ref,
               st_ref, b_ref):
    T = HGRN_TILE
    C = HGRN_CHUNK
    U = HGRN_UNROLL
    B0 = 8
    j = pl.program_id(1)

    @pl.when(j == 0)
    def _():
        st_ref[...] = jnp.zeros_like(st_ref)

    b_ref[:, 0:B0, :] = jnp.zeros((2 * U, B0, LANES), F32)

    lane = lax.broadcasted_iota(jnp.int32, (1, LANES), 1)
    h0 = lane < HEAD_DIM
    h0f = h0.astype(F32)
    h1f = 1.0 - h0f
    h0b = h0f.astype(BF16)
    h1b = h1f.astype(BF16)
    t_col = lax.broadcasted_iota(jnp.int32, (C, 1), 0)
    odd = (t_col & 1) == 1
    low4 = (t_col & 4) == 0
    bd = bd_ref[...]
    bd_mask = bd > 0

    def bcast_rows(bv, row_of_group):
        return jnp.concatenate(
            [jnp.broadcast_to(bv[pl.ds(B0 + row_of_group(g), 1), :], (8, LANES)) for g in range(C // 8)],
            axis=0)

    def ref_rows(bv, b, m):
        if m >= 4:
            return bcast_rows(bv, lambda g: (8 * g // (2 * m)) * 2 * m + m - 1)
        if m == 2:
            return jnp.where(low4, bcast_rows(bv, lambda g: 8 * g + 1), bcast_rows(bv, lambda g: 8 * g + 5))
        return jnp.where(odd, bv[pl.ds(B0 - 1, C), :], b)

    def step(it, carry):
        units = []
        for u in range(U):
            rows = pl.ds(pl.multiple_of((it * U + u) * C, C), C)
            for sp in range(2):
                units.append(dict(sp=sp, rows=rows, slot=2 * u + sp))
        for d in units:
            sp, rows = d["sp"], d["rows"]
            hf = in_ref[2 + sp, rows, :]
            d["v"] = in_ref[4 + sp, rows, :]
            d["q"] = _silu_tanh(in_ref[sp, rows, :])
            l1p = _log1p_exp_neg_abs(hf)
            log_sig = jnp.minimum(hf, 0.0) - l1p
            a1 = loglb_ref[sp:sp + 1, :]
            a2 = log1mlb_ref[sp:sp + 1, :] + log_sig
            lf = jnp.maximum(a1, a2) + _log1p_exp_neg_abs(a1 - a2)
            d["kk"] = jnp.exp(log1mlb_ref[sp:sp + 1, :] - jnp.maximum(hf, 0.0) - l1p)
            d["b"] = _exact_dot01(tri_ref[...], lf)
            b_ref[d["slot"], B0:B0 + C, :] = d["b"]
        for d in units:
            b = d["b"]
            d["blast"] = b[C - 1:C, :]
            d["qe"] = (d["q"] * jnp.exp(b)).astype(BF16)
            kd = (d["kk"] * jnp.exp(d["blast"] - b)).astype(BF16)
            d["upd"] = _dot_tn(d["v"].astype(BF16), kd)
            d["att"] = None
            d["qb"] = d["q"].astype(BF16)
            d["kb"] = d["kk"].astype(BF16)
        for sp in range(2):
            st = st_ref[sp]
            for d in units:
                if d["sp"] == sp:
                    d["o"] = _dot_nt(d["qe"], st.astype(BF16))
                    st = st * jnp.exp(d["blast"]) + jnp.where(bd_mask, d["upd"], 0.0)
            st_ref[sp] = st
        for d in units:
            d["o"] = d["o"] + _dot((d["q"] * d["kk"]).astype(BF16), bd) * d["v"]
        for i, m in enumerate(HGRN_LEVELS):
            keep = lv_ref[i, 1] > 0.5
            for d in units:
                rm = ref_rows(b_ref.at[d["slot"]], d["b"], m)
                em = jnp.exp((d["b"] - rm) * lv_ref[i, 0]).astype(BF16)
                ke = d["kb"] * em
                ks2 = jnp.concatenate([ke * h0b, ke * h1b], axis=0)
                p = _dot_nt(d["qb"] * em, ks2)
                d["att"] = jnp.where(keep, p, 0.0 if d["att"] is None else d["att"])
        for d in units:
            v = d["v"]
            vbd = jnp.concatenate([v * h0f, v * h1f], axis=0).astype(BF16)
            d["o"] = d["o"] + _dot(d["att"].astype(BF16), vbd)
        for d in units:
            sp, rows, o = d["sp"], d["rows"], d["o"]
            o2 = o * o
            ss0 = jnp.sum(jnp.where(h0, o2, 0.0), axis=-1, keepdims=True)
            ss1 = jnp.sum(jnp.where(h0, 0.0, o2), axis=-1, keepdims=True)
            r = lax.rsqrt(jnp.where(h0, ss0, ss1) * (1.0 / HEAD_DIM) + NORM_EPS)
            out_ref[sp, rows, :] = o * r * nw_ref[sp:sp + 1, :] * _silu_tanh(in_ref[6 + sp, rows, :])
        return carry

    lax.fori_loop(0, T // (C * U), step, 0)


def _hgrn_call(hgrn_in, loglb, log1mlb, nw, tri, lv, bd, batch, seq):
    T = HGRN_TILE
    nt = seq // T
    n = batch * seq
    assert HGRN_CHUNK == HEAD_DIM
    return pl.pallas_call(
        _hgrn_body,
        grid=(batch, nt),
        in_specs=[
            pl.BlockSpec((N_HGRN_SLABS, T, LANES), lambda b, j: (0, b * nt + j, 0)),
            _const_spec((2, LANES)),
            _const_spec((2, LANES)),
            _const_spec((2, LANES)),
            _const_spec((HGRN_CHUNK, HGRN_CHUNK)),
            _const_spec((len(HGRN_LEVELS), 2, HGRN_CHUNK, LANES)),
            _const_spec((LANES, LANES)),
        ],
        out_specs=pl.BlockSpec((2, T, LANES), lambda b, j: (0, b * nt + j, 0)),
        out_shape=jax.ShapeDtypeStruct((2, n, LANES), F32),
        scratch_shapes=[
            pltpu.VMEM((2, LANES, LANES), F32),
            pltpu.VMEM((2 * HGRN_UNROLL, 8 + HGRN_CHUNK, LANES), F32),
        ],
        compiler_params=pltpu.CompilerParams(
            dimension_semantics=("arbitrary", "arbitrary"), vmem_limit_bytes=VMEM_LIMIT),
        name="hgrn2",
    )(hgrn_in, loglb, log1mlb, nw, tri, lv, bd)


def _mix_ffn_body(h_ref, ya_ref, ys_ref, yh_ref, wo_ref, nw_ref, wg_ref, wu_ref, wd_ref, fw_ref,
                  o_ref, a_ref, *, final):
    parts = [ya_ref[0], ya_ref[1], ys_ref[0], ys_ref[1], ys_ref[2], ys_ref[3], yh_ref[0], yh_ref[1]]
    y = jnp.concatenate([p.astype(BF16) for p in parts], axis=1)
    x = h_ref[...] + _dot(y, wo_ref[...])
    _ffn_tail(x, nw_ref, wg_ref, wu_ref, wd_ref, fw_ref, o_ref, a_ref, final)


def _mix_ffn_call(h, ya, ys, yh, wo, nw, wg, wu, wd, fw, final):
    n = h.shape[0]
    tm = TOKEN_TILE
    return pl.pallas_call(
        functools.partial(_mix_ffn_body, final=final),
        grid=(n // tm,),
        in_specs=[
            pl.BlockSpec((tm, D_MODEL), lambda i: (i, 0)),
            pl.BlockSpec((2, tm, LANES), lambda i: (0, i, 0)),
            pl.BlockSpec((4, tm, LANES), lambda i: (0, i, 0)),
            pl.BlockSpec((2, tm, LANES), lambda i: (0, i, 0)),
            _const_spec((D_MODEL, D_MODEL)),
            _const_spec((1, D_MODEL)),
            _const_spec((D_MODEL, D_FF)),
            _const_spec((D_MODEL, D_FF)),
            _const_spec((D_FF, D_MODEL)),
            _const_spec((1, D_MODEL)),
        ],
        out_specs=pl.BlockSpec((tm, D_MODEL), lambda i: (i, 0)),
        out_shape=jax.ShapeDtypeStruct((n, D_MODEL), F32),
        scratch_shapes=[pltpu.VMEM((tm, D_FF), BF16)],
        compiler_params=pltpu.CompilerParams(
            dimension_semantics=("arbitrary",), vmem_limit_bytes=VMEM_LIMIT),
        name="outproj_ffn",
    )(h, ya, ys, yh, wo, nw, wg, wu, wd, fw)


def _permute_w_in(w):
    o = 0
    cols = {}
    for name, size in (("aq", 256), ("ak", 256), ("av", 256), ("z", 512), ("x", 512), ("B", 256),
                       ("C", 256), ("dt", 8), ("hq", 256), ("hf", 256), ("hi", 256), ("hg", 256)):
        cols[name] = w[:, o:o + size]
        o += size
    dt = jnp.pad(cols["dt"], ((0, 0), (0, LANES - SSM_HEADS)))
    order = ["ak", "av", "aq", "z", "x", "B", "C"]
    return jnp.concatenate([cols[k] for k in order] + [dt] + [cols[k] for k in ("hq", "hf", "hi", "hg")],
                           axis=1)


def _pad_lanes(v):
    return jnp.pad(v, (0, LANES - v.shape[0])).reshape(1, LANES)


def kernel(x, ffn1_norm, ffn1_w_gate, ffn1_w_up, ffn1_w_down, mix_norm, w_in, conv_w, conv_b, dt_bias, a_log, d_skip, ssm_norm, hgrn_lb_logits, hgrn_norm, w_out, ffn2_norm, ffn2_w_gate, ffn2_w_up, ffn2_w_down, final_norm):
    batch, seq, d = x.shape
    depth = w_in.shape[0]
    assert d == D_MODEL and seq % ATTN_TILE == 0
    h = x.reshape(batch * seq, d)

    sm = jax.nn.softmax(hgrn_lb_logits.astype(F32), axis=0)
    lower = jnp.clip(jnp.cumsum(sm, axis=0) - sm[0], HGRN_LB_FLOOR, 1.0 - 1e-6)
    tri = jnp.asarray(np.tril(np.ones((SSM_CHUNK, SSM_CHUNK), np.float32)), BF16)
    tri_h = jnp.asarray(np.tril(np.ones((HGRN_CHUNK, HGRN_CHUNK), np.float32)), BF16)
    lv = jnp.asarray(_hgrn_level_consts())
    bd = jnp.asarray(np.kron(np.eye(2, dtype=np.float32), np.ones((HEAD_DIM, HEAD_DIM), np.float32)), BF16)
    fw = final_norm.reshape(1, d)
    attn_bias = jnp.asarray(_attn_bias())

    for layer in range(depth):
        h = _ffn_call(h, ffn1_norm[layer].reshape(1, d), ffn1_w_gate[layer].astype(BF16),
                      ffn1_w_up[layer].astype(BF16), ffn1_w_down[layer].astype(BF16), fw, False)
        attn_in, ssd_in, hgrn_in = _inproj_call(h, mix_norm[layer].reshape(1, d),
                                                _permute_w_in(w_in[layer]).astype(BF16))
        y_attn = _attn_call(attn_in, attn_bias, batch, seq)
        y_ssm = _ssd_call(
            ssd_in,
            conv_w[layer].reshape(SSM_CONV, 8, LANES), conv_b[layer].reshape(8, LANES),
            _pad_lanes(dt_bias[layer]), _pad_lanes(a_log[layer]),
            jnp.repeat(d_skip[layer], HEAD_DIM).reshape(4, LANES), ssm_norm[layer].reshape(4, LANES),
            tri, batch, seq)
        lb = lower[layer]
        y_hgrn = _hgrn_call(hgrn_in, jnp.log(lb).reshape(2, LANES), jnp.log1p(-lb).reshape(2, LANES),
                            hgrn_norm[layer].reshape(2, LANES), tri_h, lv, bd, batch, seq)
        h = _mix_ffn_call(h, y_attn, y_ssm, y_hgrn, w_out[layer].astype(BF16),
                          ffn2_norm[layer].reshape(1, d), ffn2_w_gate[layer].astype(BF16),
                          ffn2_w_up[layer].astype(BF16), ffn2_w_down[layer].astype(BF16), fw,
                          layer == depth - 1)
    return h.reshape(batch, seq, d)
```

```python
import functools

import numpy as np
import jax
import jax.numpy as jnp
from jax import lax
from jax.experimental import pallas as pl
from jax.experimental.pallas import tpu as pltpu

F32 = jnp.float32
BF16 = jnp.bfloat16

D_MODEL = 1024
HEAD_DIM = 64
LANES = 128
ATTN_WIDTH = 256
DILATED_PATTERNS = ((128, 1), (512, 4), (2048, 16))
ATTN_BLK = 128
ATTN_TILE = 2048
MASK_VALUE = -1e30
SSM_WIDTH = 512
SSM_HEADS = 8
SSM_GROUPS = 2
SSM_STATE = 128
SSM_CONV = 4
SSM_CHUNK = 128
SSM_BC = SSM_GROUPS * SSM_STATE
SSM_TILE = 1024
HGRN_WIDTH = 256
HGRN_KEY_WIDTH = 256
HGRN_CHUNK = 64
HGRN_LEVELS = (32, 16, 8, 4, 2, 1)
HGRN_TILE = 1024
HGRN_LB_FLOOR = 1e-20
D_FF = 2816
NORM_EPS = 1e-6
TOKEN_TILE = 512
COPY_ROWS = 256
ATTN_UNROLL = 4
MIXER_UNROLL = 4
HGRN_UNROLL = 8
FF_CHUNK = 256
VMEM_LIMIT = 56 * 1024 * 1024

N_ATTN_SLABS = 6
N_SSD_SLABS = 13
N_HGRN_SLABS = 8
N_IN_SLABS = N_ATTN_SLABS + N_SSD_SLABS + N_HGRN_SLABS


def _silu(x):
    return x * jax.nn.sigmoid(x)


def _silu_tanh(x):
    h = 0.5 * x
    return h + h * jnp.tanh(h)


def _log1p_exp_neg_abs(x):
    return jnp.log(1.0 + jnp.exp(-jnp.abs(x)))


def _rms_scale(x, w):
    return x * lax.rsqrt(jnp.mean(x * x, axis=-1, keepdims=True) + NORM_EPS) * w


def _dot(a, b):
    return jnp.dot(a, b, preferred_element_type=F32)


def _dot_nt(a, b):
    return lax.dot_general(a, b, (((1,), (1,)), ((), ())), preferred_element_type=F32)


def _dot_tn(a, b):
    return lax.dot_general(a, b, (((0,), (0,)), ((), ())), preferred_element_type=F32)


def _exact_dot01(g, x):
    x1 = x.astype(BF16)
    r1 = x - x1.astype(F32)
    x2 = r1.astype(BF16)
    x3 = (r1 - x2.astype(F32)).astype(BF16)
    y = _dot(g, jnp.concatenate([x1, x2, x3], axis=1))
    return y[:, 0:LANES] + y[:, LANES:2 * LANES] + y[:, 2 * LANES:3 * LANES]


def _const_spec(shape):
    nd = len(shape)
    return pl.BlockSpec(shape, lambda *_: (0,) * nd, pipeline_mode=pl.Buffered(1))


def _layer_spec(shape, layer):
    nd = len(shape)
    return pl.BlockSpec((None,) + shape, lambda *_: (layer,) + (0,) * nd, pipeline_mode=pl.Buffered(1))


def _ffn_tail(x, nw_ref, wg_ref, wu_ref, wd_ref, fw_ref, o_ref, a_ref, final):
    xn = _rms_scale(x, nw_ref[...]).astype(BF16)
    for c in range(D_FF // FF_CHUNK):
        sl = slice(c * FF_CHUNK, (c + 1) * FF_CHUNK)
        g = _dot(xn, wg_ref[:, sl])
        u = _dot(xn, wu_ref[:, sl])
        a_ref[:, sl] = (_silu(g) * u).astype(BF16)
    y = x + 0.5 * _dot(a_ref[...], wd_ref[...])
    if final:
        y = _rms_scale(y, fw_ref[...])
    o_ref[...] = y


def _ffn_body(x_ref, nw_ref, wg_ref, wu_ref, wd_ref, fw_ref, o_ref, a_ref, *, final):
    _ffn_tail(x_ref[...], nw_ref, wg_ref, wu_ref, wd_ref, fw_ref, o_ref, a_ref, final)


def _ffn_call(h, nw, wg, wu, wd, fw, layer, final):
    n = h.shape[0]
    tm = TOKEN_TILE
    return pl.pallas_call(
        functools.partial(_ffn_body, final=final),
        grid=(n // tm,),
        in_specs=[
            pl.BlockSpec((tm, D_MODEL), lambda i: (i, 0)),
            _layer_spec((1, D_MODEL), layer),
            _layer_spec((D_MODEL, D_FF), layer),
            _layer_spec((D_MODEL, D_FF), layer),
            _layer_spec((D_FF, D_MODEL), layer),
            _const_spec((1, D_MODEL)),
        ],
        out_specs=pl.BlockSpec((tm, D_MODEL), lambda i: (i, 0)),
        out_shape=jax.ShapeDtypeStruct((n, D_MODEL), F32),
        scratch_shapes=[pltpu.VMEM((tm, D_FF), BF16)],
        compiler_params=pltpu.CompilerParams(
            dimension_semantics=("arbitrary",), vmem_limit_bytes=VMEM_LIMIT),
        name="ffn",
    )(h, nw, wg, wu, wd, fw)


def _inproj_body(x_ref, nw_ref, w_ref, attn_ref, ssd_ref, hgrn_ref):
    xn = _rms_scale(x_ref[...], nw_ref[...]).astype(BF16)

    def put(slab, val):
        if slab < N_ATTN_SLABS:
            attn_ref[slab // 2, slab % 2] = val
        elif slab < N_ATTN_SLABS + N_SSD_SLABS:
            ssd_ref[slab - N_ATTN_SLABS] = val
        else:
            hgrn_ref[slab - N_ATTN_SLABS - N_SSD_SLABS] = val

    for s0 in range(0, N_IN_SLABS, 2):
        s1 = min(s0 + 2, N_IN_SLABS)
        res = _dot(xn, w_ref[:, s0 * LANES:s1 * LANES])
        for s in range(s0, s1):
            put(s, res[:, (s - s0) * LANES:(s - s0 + 1) * LANES])


def _inproj_call(h, nw, w, layer):
    n = h.shape[0]
    tm = TOKEN_TILE
    return pl.pallas_call(
        _inproj_body,
        grid=(n // tm,),
        in_specs=[
            pl.BlockSpec((tm, D_MODEL), lambda i: (i, 0)),
            _layer_spec((1, D_MODEL), layer),
            _layer_spec((D_MODEL, N_IN_SLABS * LANES), layer),
        ],
        out_specs=[
            pl.BlockSpec((3, 2, tm, LANES), lambda i: (0, 0, i, 0)),
            pl.BlockSpec((N_SSD_SLABS, tm, LANES), lambda i: (0, i, 0)),
            pl.BlockSpec((N_HGRN_SLABS, tm, LANES), lambda i: (0, i, 0)),
        ],
        out_shape=[
            jax.ShapeDtypeStruct((3, 2, n, LANES), F32),
            jax.ShapeDtypeStruct((N_SSD_SLABS, n, LANES), F32),
            jax.ShapeDtypeStruct((N_HGRN_SLABS, n, LANES), F32),
        ],
        compiler_params=pltpu.CompilerParams(
            dimension_semantics=("arbitrary",), vmem_limit_bytes=VMEM_LIMIT),
        name="inproj",
    )(h, nw, w)


def _attn_bias():
    qi = np.arange(ATTN_BLK)[:, None]
    ki = np.arange(2 * ATTN_BLK)[None, :]
    band = (ki >= qi) & (ki <= qi + ATTN_BLK)
    planes = [band, band & (ki >= ATTN_BLK)]
    out = np.stack([np.where(np.concatenate([p, p], axis=0), 0.0, MASK_VALUE) for p in planes])
    return out.astype(np.float32)


def _attn_body(in_ref, bias_ref, out_ref, kvn_ref, kv4_ref, q4_ref, acc4_ref, m4_ref, l4_ref,
               accn_ref, mn_ref, ln_ref):
    T = ATTN_TILE
    U = ATTN_UNROLL
    B = ATTN_BLK
    Q = T // 4
    G = 2 * (T // B // U)
    assert U == 4 and G == 8
    j = pl.program_id(1)

    @pl.when(j == 0)
    def _():
        def zero_prev(i, carry):
            rows = pl.ds(pl.multiple_of(i * COPY_ROWS, COPY_ROWS), COPY_ROWS)
            for t in range(2):
                for p in range(2):
                    kvn_ref[t, p, rows, :] = jnp.zeros((COPY_ROWS, LANES), F32)
                    kv4_ref[t, p, rows, :] = jnp.zeros((COPY_ROWS, LANES), F32)
            return carry

        lax.fori_loop(0, T // COPY_ROWS, zero_prev, 0)

    def load_tile(i, carry):
        rows = pl.ds(pl.multiple_of(i * B, B), B)
        r = lax.shift_right_logical(i, 2)
        a = i & 3
        src = pl.ds(a * Q + r, B, stride=4)
        dst = pl.multiple_of(r * Q + a * B, B)
        for p in range(2):
            for t in range(2):
                kvn_ref[t, p, pl.ds(pl.multiple_of(T + i * B, B), B), :] = in_ref[t, p, rows, :]
                kv4_ref[t, p, pl.ds(T + dst, B), :] = in_ref.at[t, p][src, :]
            q4_ref[p, pl.ds(dst, B), :] = in_ref.at[2, p][src, :]
        return carry

    lax.fori_loop(0, T // B, load_tile, 0)

    h0 = lax.broadcasted_iota(jnp.int32, (1, LANES), 1) < HEAD_DIM
    h0f = h0.astype(F32) * (HEAD_DIM ** -0.5)
    h1f = (HEAD_DIM ** -0.5) - h0f
    first_step = (j == 0).astype(jnp.int32)

    def block_refs(pattern, pair, it, u):
        if pattern == 2:
            rows = pl.ds(u * Q + it, B, stride=4)
            keys = [pl.ds(u * Q + it, B, stride=4), pl.ds(T + u * Q + it, B, stride=4)]
            return (q4_ref.at[pair], rows, kv4_ref, keys, first_step,
                    (acc4_ref.at[pair], m4_ref.at[pair], l4_ref.at[pair]), rows)
        if pattern == 1:
            rows = pl.ds(pl.multiple_of(u * Q + it * B, B), B)
            prev = jnp.where(it == 0, u * Q + Q - B, T + u * Q + (it - 1) * B)
            keys = [pl.ds(pl.multiple_of(prev, B), B), pl.ds(pl.multiple_of(T + u * Q + it * B, B), B)]
            return (q4_ref.at[pair], rows, kv4_ref, keys, jnp.where(it == 0, first_step, 0),
                    (acc4_ref.at[pair], m4_ref.at[pair], l4_ref.at[pair]), rows)
        n = it * U + u
        rows = pl.ds(pl.multiple_of(n * B, B), B)
        keys = [pl.ds(pl.multiple_of(T + (n - 1) * B, B), 2 * B)]
        return (in_ref.at[2, pair], rows, kvn_ref, keys, jnp.where(n == 0, first_step, 0),
                (accn_ref.at[pair], mn_ref.at[pair], ln_ref.at[pair]), rows)

    def load_rows(view, keys):
        parts = [view[k, :] for k in keys]
        return parts[0] if len(parts) == 1 else jnp.concatenate(parts, axis=0)

    def score_stage(pattern, g):
        pair = lax.shift_right_logical(g, 2)
        it = g & 3
        scores = []
        for u in range(U):
            q_view, q_rows, kv, keys, first, _, _ = block_refs(pattern, pair, it, u)
            qb = q_view[q_rows, :]
            q2 = jnp.concatenate([qb * h0f, qb * h1f], axis=0).astype(BF16)
            kb = load_rows(kv.at[0, pair], keys).astype(BF16)
            scores.append(_dot_nt(q2, kb) + bias_ref[first])
        return scores

    def output_stage(pattern, g, scores):
        pair = lax.shift_right_logical(g, 2)
        it = g & 3
        work = []
        for u in range(U):
            _, _, kv, keys, _, acc, rows = block_refs(pattern, pair, it, u)
            s = scores[u]
            m = jnp.max(s, axis=-1, keepdims=True)
            p = jnp.exp(s - m)
            l = jnp.sum(p, axis=-1, keepdims=True)
            o2 = _dot(p.astype(BF16), load_rows(kv.at[1, pair], keys).astype(BF16))
            work.append((acc, rows, jnp.where(h0, o2[0:B], o2[B:]), jnp.where(h0, m[0:B], m[B:]),
                         jnp.where(h0, l[0:B], l[B:])))
        for (acc_v, m_v, l_v), rows, o, m, l in work:
            if pattern != 2:
                m_old = m_v[rows, :]
                m_new = jnp.maximum(m_old, m)
                a = jnp.exp(m_old - m_new)
                b = jnp.exp(m - m_new)
                o = a * acc_v[rows, :] + b * o
                l = a * l_v[rows, :] + b * l
                m = m_new
            if pattern == 0:
                out_ref.at[pair][rows, :] = o / l
            else:
                acc_v[rows, :] = o
                m_v[rows, :] = m
                l_v[rows, :] = l

    def run_pattern(pattern):
        def body(g, carry):
            output_stage(pattern, g, score_stage(pattern, g))
            return carry

        lax.fori_loop(0, G, body, 0)

    run_pattern(2)
    run_pattern(1)

    def to_token_order(i, carry):
        r = lax.shift_right_logical(i, 2)
        a = i & 3
        src = pl.ds(pl.multiple_of(r * Q + a * B, B), B)
        dst = pl.ds(a * Q + r, B, stride=4)
        for p in range(2):
            accn_ref.at[p][dst, :] = acc4_ref[p, src, :]
            mn_ref.at[p][dst, :] = m4_ref[p, src, :]
            ln_ref.at[p][dst, :] = l4_ref[p, src, :]
        return carry

    lax.fori_loop(0, T // B, to_token_order, 0)
    run_pattern(0)

    def keep_tile(i, carry):
        rows = pl.ds(pl.multiple_of(i * COPY_ROWS, COPY_ROWS), COPY_ROWS)
        src = pl.ds(pl.multiple_of(T + i * COPY_ROWS, COPY_ROWS), COPY_ROWS)
        for p in range(2):
            for t in range(2):
                kvn_ref[t, p, rows, :] = kvn_ref[t, p, src, :]
                kv4_ref[t, p, rows, :] = kv4_ref[t, p, src, :]
        return carry

    lax.fori_loop(0, T // COPY_ROWS, keep_tile, 0)


def _attn_call(attn_in, bias, batch, seq):
    T = ATTN_TILE
    nc = seq // T
    n = batch * seq
    tile = pltpu.VMEM((2, T, LANES), F32)
    return pl.pallas_call(
        _attn_body,
        grid=(batch, nc),
        in_specs=[pl.BlockSpec((3, 2, T, LANES), lambda b, j: (0, 0, b * nc + j, 0)),
                  _const_spec((2, 2 * ATTN_BLK, 2 * ATTN_BLK))],
        out_specs=pl.BlockSpec((2, T, LANES), lambda b, j: (0, b * nc + j, 0)),
        out_shape=jax.ShapeDtypeStruct((2, n, LANES), F32),
        scratch_shapes=[
            pltpu.VMEM((2, 2, 2 * T, LANES), F32),
            pltpu.VMEM((2, 2, 2 * T, LANES), F32),
            tile, tile, tile, tile, tile, tile, tile,
        ],
        compiler_params=pltpu.CompilerParams(
            dimension_semantics=("arbitrary", "arbitrary"), vmem_limit_bytes=VMEM_LIMIT),
        name="dilated_attn",
    )(attn_in, bias)


def _softplus(x):
    return jnp.maximum(x, 0.0) + _log1p_exp_neg_abs(x)


def _ssd_body(in_ref, cw_ref, cb_ref, dtb_ref, alog_ref, dsk_ref, nw_ref, tri_ref, out_ref,
              xpad_ref, xc_ref, st_ref):
    T = SSM_TILE
    C = SSM_CHUNK
    U = MIXER_UNROLL
    j = pl.program_id(1)

    @pl.when(j == 0)
    def _():
        st_ref[...] = jnp.zeros_like(st_ref)
        xpad_ref[:, 0:8, :] = jnp.zeros((8, 8, LANES), F32)

    @pl.when(j > 0)
    def _():
        xpad_ref[:, 5:8, :] = xpad_ref[:, T + 5:T + 8, :]

    def conv(s, carry):
        xpad_ref[s, 8:T + 8, :] = in_ref[4 + s]
        acc = cb_ref[pl.ds(s, 1), :] + cw_ref[0, pl.ds(s, 1), :] * xpad_ref[s, 5:5 + T, :]
        for t in range(1, SSM_CONV):
            acc = acc + cw_ref[t, pl.ds(s, 1), :] * xpad_ref[s, 5 + t:5 + t + T, :]
        xc_ref[s] = _silu_tanh(acc)
        return carry

    lax.fori_loop(0, 8, conv, 0)

    h0 = lax.broadcasted_iota(jnp.int32, (1, LANES), 1) < HEAD_DIM
    ti = lax.broadcasted_iota(jnp.int32, (C, C), 0)
    si = lax.broadcasted_iota(jnp.int32, (C, C), 1)
    causal_bias = jnp.where(si <= ti, 0.0, MASK_VALUE)
    a_neg = -jnp.exp(alog_ref[...])

    def step(it, carry):
        chunks = []
        for u in range(U):
            chunks.append(dict(rows=pl.ds(pl.multiple_of((it * U + u) * C, C), C)))
        for d in chunks:
            rows = d["rows"]
            dt = _softplus(in_ref[12, rows, :] + dtb_ref[...])
            acs = _exact_dot01(tri_ref[...], dt * a_neg)
            d["acs"] = acs
            d["acs_t"] = acs.T
            d["ldt_t"] = jnp.log(dt).T
        for d in chunks:
            rows = d["rows"]
            d["bm_t"], d["cmb"], d["cb"] = [], [], []
            for g in range(SSM_GROUPS):
                bm_t = xc_ref[4 + g, rows, :].T
                cmb = xc_ref[6 + g, rows, :].astype(BF16)
                d["bm_t"].append(bm_t)
                d["cmb"].append(cmb)
                d["cb"].append(_dot(cmb, bm_t.astype(BF16)))
        for d in chunks:
            rows = d["rows"]
            d["xs"] = [xc_ref[sl, rows, :] for sl in range(4)]
            xb = [x.astype(BF16) for x in d["xs"]]
            d["yd"], d["stn"], d["ecol"], d["cdec"] = [], [], [], []
            for h in range(SSM_HEADS):
                g, sl = h // 4, h // 2
                arow = d["acs_t"][h:h + 1, :] - d["ldt_t"][h:h + 1, :]
                acol = jnp.broadcast_to(d["acs"][:, h:h + 1], (C, C))
                alast = d["acs_t"][h:h + 1, C - 1:C]
                lmat = jnp.exp(acol - arow + causal_bias)
                d["yd"].append(_dot((d["cb"][g] * lmat).astype(BF16), xb[sl]))
                d["stn"].append(_dot((d["bm_t"][g] * jnp.exp(alast - arow)).astype(BF16), xb[sl]))
                d["ecol"].append(jnp.exp(acol))
                d["cdec"].append(jnp.exp(alast))
        for sl in range(4):
            st = st_ref[sl]
            for d in chunks:
                ev, od = 2 * sl, 2 * sl + 1
                y_off = _dot(d["cmb"][sl // 2], st.astype(BF16)) * jnp.where(h0, d["ecol"][ev], d["ecol"][od])
                st = (st * jnp.where(h0, d["cdec"][ev], d["cdec"][od])
                      + jnp.where(h0, d["stn"][ev], d["stn"][od]))
                y = jnp.where(h0, d["yd"][ev], d["yd"][od]) + y_off + dsk_ref[sl:sl + 1, :] * d["xs"][sl]
                d.setdefault("ys", []).append(y * _silu_tanh(in_ref[sl, d["rows"], :]))
            st_ref[sl] = st
        for d in chunks:
            rows = d["rows"]
            for g in range(SSM_GROUPS):
                ya, yb = d["ys"][2 * g], d["ys"][2 * g + 1]
                ss = jnp.sum(ya * ya, axis=-1, keepdims=True) + jnp.sum(yb * yb, axis=-1, keepdims=True)
                r = lax.rsqrt(ss * (1.0 / (2 * LANES)) + NORM_EPS)
                out_ref[2 * g, rows, :] = ya * r * nw_ref[2 * g:2 * g + 1, :]
                out_ref[2 * g + 1, rows, :] = yb * r * nw_ref[2 * g + 1:2 * g + 2, :]
        return carry

    lax.fori_loop(0, T // (C * U), step, 0)


def _ssd_call(ssd_in, cw, cb, dtb, alog, dsk, nw, tri, batch, seq):
    T = SSM_TILE
    nt = seq // T
    n = batch * seq
    return pl.pallas_call(
        _ssd_body,
        grid=(batch, nt),
        in_specs=[
            pl.BlockSpec((N_SSD_SLABS, T, LANES), lambda b, j: (0, b * nt + j, 0)),
            _const_spec((SSM_CONV, 8, LANES)),
            _const_spec((8, LANES)),
            _const_spec((1, LANES)),
            _const_spec((1, LANES)),
            _const_spec((4, LANES)),
            _const_spec((4, LANES)),
            _const_spec((SSM_CHUNK, SSM_CHUNK)),
        ],
        out_specs=pl.BlockSpec((4, T, LANES), lambda b, j: (0, b * nt + j, 0)),
        out_shape=jax.ShapeDtypeStruct((4, n, LANES), F32),
        scratch_shapes=[
            pltpu.VMEM((8, T + 8, LANES), F32),
            pltpu.VMEM((8, T, LANES), F32),
            pltpu.VMEM((4, SSM_STATE, LANES), F32),
        ],
        compiler_params=pltpu.CompilerParams(
            dimension_semantics=("arbitrary", "arbitrary"), vmem_limit_bytes=VMEM_LIMIT),
        name="ssd",
    )(ssd_in, cw, cb, dtb, alog, dsk, nw, tri)


def _hgrn_level_consts():
    C = HGRN_CHUNK
    t = np.arange(C)[:, None]
    s = np.arange(LANES)[None, :] % HEAD_DIM
    out = np.zeros((len(HGRN_LEVELS), 2, C, LANES), np.float32)
    for i, m in enumerate(HGRN_LEVELS):
        second = ((t // m) % 2) == 1
        out[i, 0] = np.where(second, 1.0, -1.0)
        out[i, 1] = (t // (2 * m) == s // (2 * m)) & second & (((s // m) % 2) == 0)
    return out


def _hgrn_body(in_ref, loglb_ref, log1mlb_ref, nw_ref, tri_ref, lv_ref, bd_ref, out_ref,
               st_ref, b_ref):
    T = HGRN_TILE
    C = HGRN_CHUNK
    U = HGRN_UNROLL
    B0 = 8
    j = pl.program_id(1)

    @pl.when(j == 0)
    def _():
        st_ref[...] = jnp.zeros_like(st_ref)

    b_ref[:, 0:B0, :] = jnp.zeros((2 * U, B0, LANES), F32)

    lane = lax.broadcasted_iota(jnp.int32, (1, LANES), 1)
    h0 = lane < HEAD_DIM
    h0f = h0.astype(F32)
    h1f = 1.0 - h0f
    h0b = h0f.astype(BF16)
    h1b = h1f.astype(BF16)
    t_col = lax.broadcasted_iota(jnp.int32, (C, 1), 0)
    odd = (t_col & 1) == 1
    low4 = (t_col & 4) == 0
    bd = bd_ref[...]
    bd_mask = bd > 0

    def bcast_rows(bv, row_of_group):
        return jnp.concatenate(
            [jnp.broadcast_to(bv[pl.ds(B0 + row_of_group(g), 1), :], (8, LANES)) for g in range(C // 8)],
            axis=0)

    def ref_rows(bv, b, m):
        if m >= 4:
            return bcast_rows(bv, lambda g: (8 * g // (2 * m)) * 2 * m + m - 1)
        if m == 2:
            return jnp.where(low4, bcast_rows(bv, lambda g: 8 * g + 1), bcast_rows(bv, lambda g: 8 * g + 5))
        return jnp.where(odd, bv[pl.ds(B0 - 1, C), :], b)

    def step(it, carry):
        units = []
        for u in range(U):
            rows = pl.ds(pl.multiple_of((it * U + u) * C, C), C)
            for sp in range(2):
                units.append(dict(sp=sp, rows=rows, slot=2 * u + sp))
        for d in units:
            sp, rows = d["sp"], d["rows"]
            hf = in_ref[2 + sp, rows, :]
            d["v"] = in_ref[4 + sp, rows, :]
            d["q"] = _silu_tanh(in_ref[sp, rows, :])
            l1p = _log1p_exp_neg_abs(hf)
            log_sig = jnp.minimum(hf, 0.0) - l1p
            a1 = loglb_ref[sp:sp + 1, :]
            a2 = log1mlb_ref[sp:sp + 1, :] + log_sig
            lf = jnp.maximum(a1, a2) + _log1p_exp_neg_abs(a1 - a2)
            d["kk"] = jnp.exp(log1mlb_ref[sp:sp + 1, :] - jnp.maximum(hf, 0.0) - l1p)
            d["b"] = _exact_dot01(tri_ref[...], lf)
            b_ref[d["slot"], B0:B0 + C, :] = d["b"]
        for d in units:
            b = d["b"]
            d["blast"] = b[C - 1:C, :]
            d["qe"] = (d["q"] * jnp.exp(b)).astype(BF16)
            kd = (d["kk"] * jnp.exp(d["blast"] - b)).astype(BF16)
            d["upd"] = _dot_tn(d["v"].astype(BF16), kd)
            d["att"] = None
            d["qb"] = d["q"].astype(BF16)
            d["kb"] = d["kk"].astype(BF16)
        for sp in range(2):
            st = st_ref[sp]
            for d in units:
                if d["sp"] == sp:
                    d["o"] = _dot_nt(d["qe"], st.astype(BF16))
                    st = st * jnp.exp(d["blast"]) + jnp.where(bd_mask, d["upd"], 0.0)
            st_ref[sp] = st
        for d in units:
            d["o"] = d["o"] + _dot((d["q"] * d["kk"]).astype(BF16), bd) * d["v"]
        for i, m in enumerate(HGRN_LEVELS):
            keep = lv_ref[i, 1] > 0.5
            for d in units:
                rm = ref_rows(b_ref.at[d["slot"]], d["b"], m)
                em = jnp.exp((d["b"] - rm) * lv_ref[i, 0]).astype(BF16)
                ke = d["kb"] * em
                ks2 = jnp.concatenate([ke * h0b, ke * h1b], axis=0)
                p = _dot_nt(d["qb"] * em, ks2)
                d["att"] = jnp.where(keep, p, 0.0 if d["att"] is None else d["att"])
        for d in units:
            v = d["v"]
            vbd = jnp.concatenate([v * h0f, v * h1f], axis=0).astype(BF16)
            d["o"] = d["o"] + _dot(d["att"].astype(BF16), vbd)
        for d in units:
            sp, rows, o = d["sp"], d["rows"], d["o"]
            o2 = o * o
            ss0 = jnp.sum(jnp.where(h0, o2, 0.0), axis=-1, keepdims=True)
            ss1 = jnp.sum(jnp.where(h0, 0.0, o2), axis=-1, keepdims=True)
            r = lax.rsqrt(jnp.where(h0, ss0, ss1) * (1.0 / HEAD_DIM) + NORM_EPS)
            out_ref[sp, rows, :] = o * r * nw_ref[sp:sp + 1, :] * _silu_tanh(in_ref[6 + sp, rows, :])
        return carry

    lax.fori_loop(0, T // (C * U), step, 0)


def _hgrn_call(hgrn_in, loglb, log1mlb, nw, tri, lv, bd, batch, seq):
    T = HGRN_TILE
    nt = seq // T
    n = batch * seq
    assert HGRN_CHUNK == HEAD_DIM
    return pl.pallas_call(
        _hgrn_body,
        grid=(batch, nt),
        in_specs=[
            pl.BlockSpec((N_HGRN_SLABS, T, LANES), lambda b, j: (0, b * nt + j, 0)),
            _const_spec((2, LANES)),
            _const_spec((2, LANES)),
            _const_spec((2, LANES)),
            _const_spec((HGRN_CHUNK, HGRN_CHUNK)),
            _const_spec((len(HGRN_LEVELS), 2, HGRN_CHUNK, LANES)),
            _const_spec((LANES, LANES)),
        ],
        out_specs=pl.BlockSpec((2, T, LANES), lambda b, j: (0, b * nt + j, 0)),
        out_shape=jax.ShapeDtypeStruct((2, n, LANES), F32),
        scratch_shapes=[
            pltpu.VMEM((2, LANES, LANES), F32),
            pltpu.VMEM((2 * HGRN_UNROLL, 8 + HGRN_CHUNK, LANES), F32),
        ],
        compiler_params=pltpu.CompilerParams(
            dimension_semantics=("arbitrary", "arbitrary"), vmem_limit_bytes=VMEM_LIMIT),
        name="hgrn2",
    )(hgrn_in, loglb, log1mlb, nw, tri, lv, bd)


def _mix_ffn_body(h_ref, ya_ref, ys_ref, yh_ref, wo_ref, nw_ref, wg_ref, wu_ref, wd_ref, fw_ref,
                  o_ref, a_ref, *, final):
    parts = [ya_ref[0], ya_ref[1], ys_ref[0], ys_ref[1], ys_ref[2], ys_ref[3], yh_ref[0], yh_ref[1]]
    y = jnp.concatenate([p.astype(BF16) for p in parts], axis=1)
    x = h_ref[...] + _dot(y, wo_ref[...])
    _ffn_tail(x, nw_ref, wg_ref, wu_ref, wd_ref, fw_ref, o_ref, a_ref, final)


def _mix_ffn_call(h, ya, ys, yh, wo, nw, wg, wu, wd, fw, layer, final):
    n = h.shape[0]
    tm = TOKEN_TILE
    return pl.pallas_call(
        functools.partial(_mix_ffn_body, final=final),
        grid=(n // tm,),
        in_specs=[
            pl.BlockSpec((tm, D_MODEL), lambda i: (i, 0)),
            pl.BlockSpec((2, tm, LANES), lambda i: (0, i, 0)),
            pl.BlockSpec((4, tm, LANES), lambda i: (0, i, 0)),
            pl.BlockSpec((2, tm, LANES), lambda i: (0, i, 0)),
            _layer_spec((D_MODEL, D_MODEL), layer),
            _layer_spec((1, D_MODEL), layer),
            _layer_spec((D_MODEL, D_FF), layer),
            _layer_spec((D_MODEL, D_FF), layer),
            _layer_spec((D_FF, D_MODEL), layer),
            _const_spec((1, D_MODEL)),
        ],
        out_specs=pl.BlockSpec((tm, D_MODEL), lambda i: (i, 0)),
        out_shape=jax.ShapeDtypeStruct((n, D_MODEL), F32),
        scratch_shapes=[pltpu.VMEM((tm, D_FF), BF16)],
        compiler_params=pltpu.CompilerParams(
            dimension_semantics=("arbitrary",), vmem_limit_bytes=VMEM_LIMIT),
        name="outproj_ffn",
    )(h, ya, ys, yh, wo, nw, wg, wu, wd, fw)


def _permute_w_in(w):
    o = 0
    cols = {}
    for name, size in (("aq", 256), ("ak", 256), ("av", 256), ("z", 512), ("x", 512), ("B", 256),
                       ("C", 256), ("dt", 8), ("hq", 256), ("hf", 256), ("hi", 256), ("hg", 256)):
        cols[name] = w[..., o:o + size]
        o += size
    dt = jnp.pad(cols["dt"], ((0, 0), (0, 0), (0, LANES - SSM_HEADS)))
    order = ["ak", "av", "aq", "z", "x", "B", "C"]
    return jnp.concatenate([cols[k] for k in order] + [dt] + [cols[k] for k in ("hq", "hf", "hi", "hg")],
                           axis=-1)


def _pad_lanes(v):
    return jnp.pad(v, (0, LANES - v.shape[0])).reshape(1, LANES)


def kernel(x, ffn1_norm, ffn1_w_gate, ffn1_w_up, ffn1_w_down, mix_norm, w_in, conv_w, conv_b, dt_bias, a_log, d_skip, ssm_norm, hgrn_lb_logits, hgrn_norm, w_out, ffn2_norm, ffn2_w_gate, ffn2_w_up, ffn2_w_down, final_norm):
    batch, seq, d = x.shape
    depth = w_in.shape[0]
    assert d == D_MODEL and seq % ATTN_TILE == 0
    h = x.reshape(batch * seq, d)

    sm = jax.nn.softmax(hgrn_lb_logits.astype(F32), axis=0)
    lower = jnp.clip(jnp.cumsum(sm, axis=0) - sm[0], HGRN_LB_FLOOR, 1.0 - 1e-6)
    tri = jnp.asarray(np.tril(np.ones((SSM_CHUNK, SSM_CHUNK), np.float32)), BF16)
    tri_h = jnp.asarray(np.tril(np.ones((HGRN_CHUNK, HGRN_CHUNK), np.float32)), BF16)
    lv = jnp.asarray(_hgrn_level_consts())
    bd = jnp.asarray(np.kron(np.eye(2, dtype=np.float32), np.ones((HEAD_DIM, HEAD_DIM), np.float32)), BF16)
    fw = final_norm.reshape(1, d)
    attn_bias = jnp.asarray(_attn_bias())

    norms = [w.reshape(depth, 1, d) for w in (ffn1_norm, mix_norm, ffn2_norm)]
    ffn1_w = [w.astype(BF16) for w in (ffn1_w_gate, ffn1_w_up, ffn1_w_down)]
    ffn2_w = [w.astype(BF16) for w in (ffn2_w_gate, ffn2_w_up, ffn2_w_down)]
    w_in_b = _permute_w_in(w_in).astype(BF16)
    w_out_b = w_out.astype(BF16)

    for layer in range(depth):
        h = _ffn_call(h, norms[0], *ffn1_w, fw, layer, False)
        attn_in, ssd_in, hgrn_in = _inproj_call(h, norms[1], w_in_b, layer)
        y_attn = _attn_call(attn_in, attn_bias, batch, seq)
        y_ssm = _ssd_call(
            ssd_in,
            conv_w[layer].reshape(SSM_CONV, 8, LANES), conv_b[layer].reshape(8, LANES),
            _pad_lanes(dt_bias[layer]), _pad_lanes(a_log[layer]),
            jnp.repeat(d_skip[layer], HEAD_DIM).reshape(4, LANES), ssm_norm[layer].reshape(4, LANES),
            tri, batch, seq)
        lb = lower[layer]
        y_hgrn = _hgrn_call(hgrn_in, jnp.log(lb).reshape(2, LANES), jnp.log1p(-lb).reshape(2, LANES),
                            hgrn_norm[layer].reshape(2, LANES), tri_h, lv, bd, batch, seq)
        h = _mix_ffn_call(h, y_attn, y_ssm, y_hgrn, w_out_b, norms[2], *ffn2_w, fw, layer,
                          layer == depth - 1)
    return h.reshape(batch, seq, d)
```

```python
import functools

import numpy as np
import jax
import jax.numpy as jnp
from jax import lax
from jax.experimental import pallas as pl
from jax.experimental.pallas import tpu as pltpu

F32 = jnp.float32
BF16 = jnp.bfloat16

D_MODEL = 1024
HEAD_DIM = 64
LANES = 128
ATTN_WIDTH = 256
DILATED_PATTERNS = ((128, 1), (512, 4), (2048, 16))
ATTN_BLK = 128
ATTN_TILE = 2048
MASK_VALUE = -1e30
SSM_WIDTH = 512
SSM_HEADS = 8
SSM_GROUPS = 2
SSM_STATE = 128
SSM_CONV = 4
SSM_CHUNK = 128
SSM_BC = SSM_GROUPS * SSM_STATE
SSM_TILE = 1024
HGRN_WIDTH = 256
HGRN_KEY_WIDTH = 256
HGRN_CHUNK = 64
HGRN_LEVELS = (32, 16, 8, 4, 2, 1)
HGRN_TILE = 1024
HGRN_LB_FLOOR = 1e-20
D_FF = 2816
NORM_EPS = 1e-6
TOKEN_TILE = 512
POST_ROWS = 64
COPY_ROWS = 256
ATTN_UNROLL = 8
MIXER_UNROLL = 4
HGRN_UNROLL = 8
FF_CHUNK = 256
VMEM_LIMIT = 56 * 1024 * 1024

N_ATTN_SLABS = 6
N_SSD_SLABS = 13
N_HGRN_SLABS = 10
N_IN_SLABS = 27


def _silu(x):
    return x * jax.nn.sigmoid(x)


def _silu_tanh(x):
    h = 0.5 * x
    return h + h * jnp.tanh(h)


def _log1p_exp_neg_abs(x):
    return jnp.log(1.0 + jnp.exp(-jnp.abs(x)))


def _softplus(x):
    return jnp.maximum(x, 0.0) + _log1p_exp_neg_abs(x)


def _rms_scale(x, w):
    return x * lax.rsqrt(jnp.mean(x * x, axis=-1, keepdims=True) + NORM_EPS) * w


def _dot(a, b):
    return jnp.dot(a, b, preferred_element_type=F32)


def _dot_nt(a, b):
    return lax.dot_general(a, b, (((1,), (1,)), ((), ())), preferred_element_type=F32)


def _dot_tn(a, b):
    return lax.dot_general(a, b, (((0,), (0,)), ((), ())), preferred_element_type=F32)


def _exact_dot01(g, x):
    x1 = x.astype(BF16)
    r1 = x - x1.astype(F32)
    x2 = r1.astype(BF16)
    x3 = (r1 - x2.astype(F32)).astype(BF16)
    y = _dot(g, jnp.concatenate([x1, x2, x3], axis=1))
    return y[:, 0:LANES] + y[:, LANES:2 * LANES] + y[:, 2 * LANES:3 * LANES]


def _const_spec(shape):
    nd = len(shape)
    return pl.BlockSpec(shape, lambda *_: (0,) * nd, pipeline_mode=pl.Buffered(1))


def _layer_spec(shape, layer):
    nd = len(shape)
    return pl.BlockSpec((None,) + shape, lambda *_: (layer,) + (0,) * nd, pipeline_mode=pl.Buffered(1))


def _ffn_tail(x, nw_ref, wg_ref, wu_ref, wd_ref, fw_ref, o_ref, a_ref, final):
    xn = _rms_scale(x, nw_ref[...]).astype(BF16)
    for c in range(D_FF // FF_CHUNK):
        sl = slice(c * FF_CHUNK, (c + 1) * FF_CHUNK)
        g = _dot(xn, wg_ref[:, sl])
        u = _dot(xn, wu_ref[:, sl])
        a_ref[:, sl] = (_silu(g) * u).astype(BF16)
    y = x + 0.5 * _dot(a_ref[...], wd_ref[...])
    if final:
        y = _rms_scale(y, fw_ref[...])
    o_ref[...] = y


def _ffn_body(x_ref, nw_ref, wg_ref, wu_ref, wd_ref, fw_ref, o_ref, a_ref, *, final):
    _ffn_tail(x_ref[...], nw_ref, wg_ref, wu_ref, wd_ref, fw_ref, o_ref, a_ref, final)


def _ffn_call(h, nw, wg, wu, wd, fw, layer, final):
    n = h.shape[0]
    tm = TOKEN_TILE
    return pl.pallas_call(
        functools.partial(_ffn_body, final=final),
        grid=(n // tm,),
        in_specs=[
            pl.BlockSpec((tm, D_MODEL), lambda i: (i, 0)),
            _layer_spec((1, D_MODEL), layer),
            _layer_spec((D_MODEL, D_FF), layer),
            _layer_spec((D_MODEL, D_FF), layer),
            _layer_spec((D_FF, D_MODEL), layer),
            _const_spec((1, D_MODEL)),
        ],
        out_specs=pl.BlockSpec((tm, D_MODEL), lambda i: (i, 0)),
        out_shape=jax.ShapeDtypeStruct((n, D_MODEL), F32),
        scratch_shapes=[pltpu.VMEM((tm, D_FF), BF16)],
        compiler_params=pltpu.CompilerParams(
            dimension_semantics=("arbitrary",), vmem_limit_bytes=VMEM_LIMIT),
        name="ffn",
    )(h, nw, wg, wu, wd, fw)


def _inproj_body(x_ref, nw_ref, w_ref, cw_ref, cb_ref, dtb_ref, loglb_ref, log1mlb_ref,
                 attn_ref, ssd_ref, hgrn_ref, xpad_ref, *, tiles_per_seq):
    tm = TOKEN_TILE
    first_tile = pl.program_id(0) % tiles_per_seq == 0

    @pl.when(first_tile)
    def _():
        xpad_ref[:, 0:8, :] = jnp.zeros((8, 8, LANES), F32)

    @pl.when(jnp.logical_not(first_tile))
    def _():
        xpad_ref[:, 5:8, :] = xpad_ref[:, tm + 5:tm + 8, :]

    xn = _rms_scale(x_ref[...], nw_ref[...]).astype(BF16)
    row_blocks = [slice(r0, r0 + POST_ROWS) for r0 in range(0, tm, POST_ROWS)]
    after = []

    def project(s0, n, store, post=()):
        res = _dot(xn, w_ref[:, s0 * LANES:(s0 + n) * LANES])
        for k in range(n):
            store(k, res[:, k * LANES:(k + 1) * LANES])
        for fn in after:
            fn()
        after[:] = post

    def silu_in_place(ref, idx):
        def run():
            for rb in row_blocks:
                ref[idx, rb, :] = _silu_tanh(ref[idx, rb, :])
        return run

    def conv_silu(c):
        def run():
            for rb in row_blocks:
                acc = cb_ref[c:c + 1, :]
                for t in range(SSM_CONV):
                    acc = acc + cw_ref[t, c:c + 1, :] * xpad_ref[c, 5 + t + rb.start:5 + t + rb.stop, :]
                ssd_ref[4 + c, rb, :] = _silu_tanh(acc)
        return run

    def forget_gate(sp):
        def run():
            for rb in row_blocks:
                hf = hgrn_ref[2 + sp, rb, :]
                l1p = _log1p_exp_neg_abs(hf)
                a1 = loglb_ref[sp:sp + 1, :]
                a2 = log1mlb_ref[sp:sp + 1, :] + jnp.minimum(hf, 0.0) - l1p
                hgrn_ref[2 + sp, rb, :] = jnp.maximum(a1, a2) + _log1p_exp_neg_abs(a1 - a2)
                hgrn_ref[4 + sp, rb, :] = jnp.exp(log1mlb_ref[sp:sp + 1, :] - jnp.maximum(hf, 0.0) - l1p)
        return run

    def softplus_dt():
        for rb in row_blocks:
            ssd_ref[12, rb, :] = _softplus(ssd_ref[12, rb, :] + dtb_ref[...])

    def to_xpad(c0):
        def store(k, val):
            xpad_ref[c0 + k, 8:tm + 8, :] = val
        return store

    def to_slab(ref, i0):
        def store(k, val):
            ref[i0 + k] = val
        return store

    def to_attn(t):
        def store(k, val):
            attn_ref[t, k] = val
        return store

    for c0 in range(0, 8, 2):
        project(c0, 2, to_xpad(c0), [conv_silu(c0), conv_silu(c0 + 1)])
    for s in range(0, 4, 2):
        project(8 + s, 2, to_slab(ssd_ref, s), [silu_in_place(ssd_ref, s), silu_in_place(ssd_ref, s + 1)])
    project(12, 2, to_slab(hgrn_ref, 0), [silu_in_place(hgrn_ref, 0), silu_in_place(hgrn_ref, 1)])
    project(14, 2, to_slab(hgrn_ref, 2), [forget_gate(0), forget_gate(1)])
    project(16, 2, to_slab(hgrn_ref, 8), [silu_in_place(hgrn_ref, 8), silu_in_place(hgrn_ref, 9)])
    project(18, 2, to_slab(hgrn_ref, 6))
    project(20, 1, to_slab(ssd_ref, 12), [softplus_dt])
    for t in range(3):
        project(21 + 2 * t, 2, to_attn(t))
    for fn in after:
        fn()


def _inproj_call(h, nw, w, cw, cb, dtb, loglb, log1mlb, layer, seq):
    n = h.shape[0]
    tm = TOKEN_TILE
    return pl.pallas_call(
        functools.partial(_inproj_body, tiles_per_seq=seq // tm),
        grid=(n // tm,),
        in_specs=[
            pl.BlockSpec((tm, D_MODEL), lambda i: (i, 0)),
            _layer_spec((1, D_MODEL), layer),
            _layer_spec((D_MODEL, N_IN_SLABS * LANES), layer),
            _const_spec((SSM_CONV, 8, LANES)),
            _const_spec((8, LANES)),
            _const_spec((1, LANES)),
            _const_spec((2, LANES)),
            _const_spec((2, LANES)),
        ],
        out_specs=[
            pl.BlockSpec((3, 2, tm, LANES), lambda i: (0, 0, i, 0)),
            pl.BlockSpec((N_SSD_SLABS, tm, LANES), lambda i: (0, i, 0)),
            pl.BlockSpec((N_HGRN_SLABS, tm, LANES), lambda i: (0, i, 0)),
        ],
        out_shape=[
            jax.ShapeDtypeStruct((3, 2, n, LANES), F32),
            jax.ShapeDtypeStruct((N_SSD_SLABS, n, LANES), F32),
            jax.ShapeDtypeStruct((N_HGRN_SLABS, n, LANES), F32),
        ],
        scratch_shapes=[pltpu.VMEM((8, tm + 8, LANES), F32)],
        compiler_params=pltpu.CompilerParams(
            dimension_semantics=("arbitrary",), vmem_limit_bytes=VMEM_LIMIT),
        name="inproj",
    )(h, nw, w, cw, cb, dtb, loglb, log1mlb)


def _attn_bias():
    qi = np.arange(ATTN_BLK)[:, None]
    ki = np.arange(2 * ATTN_BLK)[None, :]
    band = (ki >= qi) & (ki <= qi + ATTN_BLK)
    planes = [band, band & (ki >= ATTN_BLK)]
    out = np.stack([np.where(np.concatenate([p, p], axis=0), 0.0, MASK_VALUE) for p in planes])
    return out.astype(np.float32)


def _attn_body(in_ref, bias_ref, out_ref, kvn_ref, kv4_ref, q4_ref, acc4_ref, m4_ref, l4_ref,
               accn_ref, mn_ref, ln_ref):
    T = ATTN_TILE
    U = ATTN_UNROLL
    B = ATTN_BLK
    Q = T // 4
    GP = T // B // U
    G = 2 * GP
    assert U % 4 == 0 and GP * U * B == T
    j = pl.program_id(1)

    cur4 = (j & 1) * T
    prev4 = T - cur4

    @pl.when(j == 0)
    def _():
        for t in range(2):
            for p in range(2):
                kvn_ref[t, p, 0:B, :] = jnp.zeros((B, LANES), F32)

        def zero_prev(i, carry):
            rows = pl.ds(pl.multiple_of(T + i * COPY_ROWS, COPY_ROWS), COPY_ROWS)
            for t in range(2):
                for p in range(2):
                    kv4_ref[t, p, rows, :] = jnp.zeros((COPY_ROWS, LANES), F32)
            return carry

        lax.fori_loop(0, T // COPY_ROWS, zero_prev, 0)

    def load_tile(i, carry):
        rows = pl.ds(pl.multiple_of(i * B, B), B)
        r = lax.shift_right_logical(i, 2)
        a = i & 3
        src = pl.ds(a * Q + r, B, stride=4)
        dst = r * Q + a * B
        for p in range(2):
            for t in range(2):
                kvn_ref[t, p, pl.ds(pl.multiple_of(B + i * B, B), B), :] = in_ref[t, p, rows, :]
                kv4_ref[t, p, pl.ds(pl.multiple_of(cur4 + dst, B), B), :] = in_ref.at[t, p][src, :]
            q4_ref[p, pl.ds(pl.multiple_of(dst, B), B), :] = in_ref.at[2, p][src, :]
        return carry

    lax.fori_loop(0, T // B, load_tile, 0)

    h0 = lax.broadcasted_iota(jnp.int32, (1, LANES), 1) < HEAD_DIM
    h0f = h0.astype(F32) * (HEAD_DIM ** -0.5)
    h1f = (HEAD_DIM ** -0.5) - h0f
    first_step = (j == 0).astype(jnp.int32)

    def block_refs(pattern, pair, it, u):
        n = it * U + u
        r = u % 4
        a = it * (U // 4) + u // 4
        if pattern == 2:
            rows = pl.ds(r * Q + a, B, stride=4)
            keys = [pl.ds(prev4 + r * Q + a, B, stride=4), pl.ds(cur4 + r * Q + a, B, stride=4)]
            return (q4_ref.at[pair], rows, kv4_ref, keys, first_step,
                    (acc4_ref.at[pair], m4_ref.at[pair], l4_ref.at[pair]), rows)
        if pattern == 1:
            rows = pl.ds(pl.multiple_of(r * Q + a * B, B), B)
            prev = jnp.where(a == 0, prev4 + r * Q + Q - B, cur4 + r * Q + (a - 1) * B)
            keys = [pl.ds(pl.multiple_of(prev, B), B), pl.ds(pl.multiple_of(cur4 + r * Q + a * B, B), B)]
            return (q4_ref.at[pair], rows, kv4_ref, keys, jnp.where(a == 0, first_step, 0),
                    (acc4_ref.at[pair], m4_ref.at[pair], l4_ref.at[pair]), rows)
        rows = pl.ds(pl.multiple_of(n * B, B), B)
        keys = [pl.ds(pl.multiple_of(n * B, B), 2 * B)]
        return (in_ref.at[2, pair], rows, kvn_ref, keys, jnp.where(n == 0, first_step, 0),
                (accn_ref.at[pair], mn_ref.at[pair], ln_ref.at[pair]), rows)

    def load_rows(view, keys):
        parts = [view[k, :] for k in keys]
        return parts[0] if len(parts) == 1 else jnp.concatenate(parts, axis=0)

    def score_stage(pattern, g):
        pair = g // GP
        it = g % GP
        scores = []
        for u in range(U):
            q_view, q_rows, kv, keys, first, _, _ = block_refs(pattern, pair, it, u)
            qb = q_view[q_rows, :]
            q2 = jnp.concatenate([qb * h0f, qb * h1f], axis=0).astype(BF16)
            kb = load_rows(kv.at[0, pair], keys).astype(BF16)
            scores.append(_dot_nt(q2, kb) + bias_ref[first])
        return scores

    def output_stage(pattern, g, scores):
        pair = g // GP
        it = g % GP
        work = []
        for u in range(U):
            _, _, kv, keys, _, acc, rows = block_refs(pattern, pair, it, u)
            s = scores[u]
            m = jnp.max(s, axis=-1, keepdims=True)
            p = jnp.exp(s - m)
            l = jnp.sum(p, axis=-1, keepdims=True)
            o2 = _dot(p.astype(BF16), load_rows(kv.at[1, pair], keys).astype(BF16))
            work.append((acc, rows, jnp.where(h0, o2[0:B], o2[B:]), jnp.where(h0, m[0:B], m[B:]),
                         jnp.where(h0, l[0:B], l[B:])))
        for (acc_v, m_v, l_v), rows, o, m, l in work:
            if pattern != 2:
                m_old = m_v[rows, :]
                m_new = jnp.maximum(m_old, m)
                a = jnp.exp(m_old - m_new)
                b = jnp.exp(m - m_new)
                o = a * acc_v[rows, :] + b * o
                l = a * l_v[rows, :] + b * l
                m = m_new
            if pattern == 0:
                out_ref.at[pair][rows, :] = o / l
            else:
                acc_v[rows, :] = o
                m_v[rows, :] = m
                l_v[rows, :] = l

    def run_pattern(pattern):
        def body(g, carry):
            output_stage(pattern, g, score_stage(pattern, g))
            return carry

        lax.fori_loop(0, G, body, 0)

    run_pattern(2)
    run_pattern(1)

    def to_token_order(i, carry):
        r = lax.shift_right_logical(i, 2)
        a = i & 3
        src = pl.ds(pl.multiple_of(r * Q + a * B, B), B)
        dst = pl.ds(a * Q + r, B, stride=4)
        for p in range(2):
            accn_ref.at[p][dst, :] = acc4_ref[p, src, :]
            mn_ref.at[p][dst, :] = m4_ref[p, src, :]
            ln_ref.at[p][dst, :] = l4_ref[p, src, :]
        return carry

    lax.fori_loop(0, T // B, to_token_order, 0)
    run_pattern(0)

    for t in range(2):
        for p in range(2):
            kvn_ref[t, p, 0:B, :] = kvn_ref[t, p, T:T + B, :]


def _attn_call(attn_in, bias, batch, seq):
    T = ATTN_TILE
    nc = seq // T
    n = batch * seq
    tile = pltpu.VMEM((2, T, LANES), F32)
    return pl.pallas_call(
        _attn_body,
        grid=(batch, nc),
        in_specs=[pl.BlockSpec((3, 2, T, LANES), lambda b, j: (0, 0, b * nc + j, 0)),
                  _const_spec((2, 2 * ATTN_BLK, 2 * ATTN_BLK))],
        out_specs=pl.BlockSpec((2, T, LANES), lambda b, j: (0, b * nc + j, 0)),
        out_shape=jax.ShapeDtypeStruct((2, n, LANES), F32),
        scratch_shapes=[
            pltpu.VMEM((2, 2, T + ATTN_BLK, LANES), F32),
            pltpu.VMEM((2, 2, 2 * T, LANES), F32),
            tile, tile, tile, tile, tile, tile, tile,
        ],
        compiler_params=pltpu.CompilerParams(
            dimension_semantics=("arbitrary", "arbitrary"), vmem_limit_bytes=VMEM_LIMIT),
        name="dilated_attn",
    )(attn_in, bias)


def _ssd_body(in_ref, alog_ref, dsk_ref, nw_ref, tri_ref, out_ref, st_ref):
    T = SSM_TILE
    C = SSM_CHUNK
    U = MIXER_UNROLL
    j = pl.program_id(1)

    @pl.when(j == 0)
    def _():
        st_ref[...] = jnp.zeros_like(st_ref)

    h0 = lax.broadcasted_iota(jnp.int32, (1, LANES), 1) < HEAD_DIM
    ti = lax.broadcasted_iota(jnp.int32, (C, C), 0)
    si = lax.broadcasted_iota(jnp.int32, (C, C), 1)
    causal_bias = jnp.where(si <= ti, 0.0, MASK_VALUE)
    a_neg = -jnp.exp(alog_ref[...])

    def step(it, carry):
        chunks = []
        for u in range(U):
            chunks.append(dict(rows=pl.ds(pl.multiple_of((it * U + u) * C, C), C)))
        for d in chunks:
            rows = d["rows"]
            dt = in_ref[12, rows, :]
            acs = _exact_dot01(tri_ref[...], dt * a_neg)
            d["acs"] = acs
            d["acs_t"] = acs.T
            d["ldt_t"] = jnp.log(dt).T
        for d in chunks:
            rows = d["rows"]
            d["bm_t"], d["cmb"], d["cb"] = [], [], []
            for g in range(SSM_GROUPS):
                bm_t = in_ref[8 + g, rows, :].T
                cmb = in_ref[10 + g, rows, :].astype(BF16)
                d["bm_t"].append(bm_t)
                d["cmb"].append(cmb)
                d["cb"].append(_dot(cmb, bm_t.astype(BF16)))
        for d in chunks:
            rows = d["rows"]
            d["xs"] = [in_ref[4 + sl, rows, :] for sl in range(4)]
            xb = [x.astype(BF16) for x in d["xs"]]
            d["yd"], d["stn"], d["ecol"], d["cdec"] = [], [], [], []
            for h in range(SSM_HEADS):
                g, sl = h // 4, h // 2
                arow = d["acs_t"][h:h + 1, :] - d["ldt_t"][h:h + 1, :]
                acol = jnp.broadcast_to(d["acs"][:, h:h + 1], (C, C))
                alast = d["acs_t"][h:h + 1, C - 1:C]
                lmat = jnp.exp(acol - arow + causal_bias)
                d["yd"].append(_dot((d["cb"][g] * lmat).astype(BF16), xb[sl]))
                d["stn"].append(_dot((d["bm_t"][g] * jnp.exp(alast - arow)).astype(BF16), xb[sl]))
                d["ecol"].append(jnp.exp(acol))
                d["cdec"].append(jnp.exp(alast))
        for sl in range(4):
            st = st_ref[sl]
            for d in chunks:
                ev, od = 2 * sl, 2 * sl + 1
                y_off = _dot(d["cmb"][sl // 2], st.astype(BF16)) * jnp.where(h0, d["ecol"][ev], d["ecol"][od])
                st = (st * jnp.where(h0, d["cdec"][ev], d["cdec"][od])
                      + jnp.where(h0, d["stn"][ev], d["stn"][od]))
                y = jnp.where(h0, d["yd"][ev], d["yd"][od]) + y_off + dsk_ref[sl:sl + 1, :] * d["xs"][sl]
                d.setdefault("ys", []).append(y * in_ref[sl, d["rows"], :])
            st_ref[sl] = st
        for d in chunks:
            rows = d["rows"]
            for g in range(SSM_GROUPS):
                ya, yb = d["ys"][2 * g], d["ys"][2 * g + 1]
                ss = jnp.sum(ya * ya, axis=-1, keepdims=True) + jnp.sum(yb * yb, axis=-1, keepdims=True)
                r = lax.rsqrt(ss * (1.0 / (2 * LANES)) + NORM_EPS)
                out_ref[2 * g, rows, :] = ya * r * nw_ref[2 * g:2 * g + 1, :]
                out_ref[2 * g + 1, rows, :] = yb * r * nw_ref[2 * g + 1:2 * g + 2, :]
        return carry

    lax.fori_loop(0, T // (C * U), step, 0)


def _ssd_call(ssd_in, alog, dsk, nw, tri, batch, seq):
    T = SSM_TILE
    nt = seq // T
    n = batch * seq
    return pl.pallas_call(
        _ssd_body,
        grid=(batch, nt),
        in_specs=[
            pl.BlockSpec((N_SSD_SLABS, T, LANES), lambda b, j: (0, b * nt + j, 0)),
            _const_spec((1, LANES)),
            _const_spec((4, LANES)),
            _const_spec((4, LANES)),
            _const_spec((SSM_CHUNK, SSM_CHUNK)),
        ],
        out_specs=pl.BlockSpec((4, T, LANES), lambda b, j: (0, b * nt + j, 0)),
        out_shape=jax.ShapeDtypeStruct((4, n, LANES), F32),
        scratch_shapes=[pltpu.VMEM((4, SSM_STATE, LANES), F32)],
        compiler_params=pltpu.CompilerParams(
            dimension_semantics=("arbitrary", "arbitrary"), vmem_limit_bytes=VMEM_LIMIT),
        name="ssd",
    )(ssd_in, alog, dsk, nw, tri)


def _hgrn_level_consts():
    C = HGRN_CHUNK
    t = np.arange(C)[:, None]
    s = np.arange(LANES)[None, :] % HEAD_DIM
    out = np.zeros((len(HGRN_LEVELS), 2, C, LANES), np.float32)
    for i, m in enumerate(HGRN_LEVELS):
        second = ((t // m) % 2) == 1
        out[i, 0] = np.where(second, 1.0, -1.0)
        out[i, 1] = (t // (2 * m) == s // (2 * m)) & second & (((s // m) % 2) == 0)
    return out


def _hgrn_body(in_ref, nw_ref, tri_ref, lv_ref, bd_ref, out_ref, st_ref, b_ref):
    T = HGRN_TILE
    C = HGRN_CHUNK
    U = HGRN_UNROLL
    B0 = 8
    j = pl.program_id(1)

    @pl.when(j == 0)
    def _():
        st_ref[...] = jnp.zeros_like(st_ref)

    b_ref[:, 0:B0, :] = jnp.zeros((2 * U, B0, LANES), F32)

    lane = lax.broadcasted_iota(jnp.int32, (1, LANES), 1)
    h0 = lane < HEAD_DIM
    h0f = h0.astype(F32)
    h1f = 1.0 - h0f
    h0b = h0f.astype(BF16)
    h1b = h1f.astype(BF16)
    t_col = lax.broadcasted_iota(jnp.int32, (C, 1), 0)
    odd = (t_col & 1) == 1
    low4 = (t_col & 4) == 0
    bd = bd_ref[...]
    bd_mask = bd > 0

    def bcast_rows(bv, row_of_group):
        return jnp.concatenate(
            [jnp.broadcast_to(bv[pl.ds(B0 + row_of_group(g), 1), :], (8, LANES)) for g in range(C // 8)],
            axis=0)

    def ref_rows(bv, b, m):
        if m >= 4:
            return bcast_rows(bv, lambda g: (8 * g // (2 * m)) * 2 * m + m - 1)
        if m == 2:
            return jnp.where(low4, bcast_rows(bv, lambda g: 8 * g + 1), bcast_rows(bv, lambda g: 8 * g + 5))
        return jnp.where(odd, bv[pl.ds(B0 - 1, C), :], b)

    def step(it, carry):
        units = []
        for u in range(U):
            rows = pl.ds(pl.multiple_of((it * U + u) * C, C), C)
            for sp in range(2):
                units.append(dict(sp=sp, rows=rows, slot=2 * u + sp))
        for d in units:
            sp, rows = d["sp"], d["rows"]
            d["q"] = in_ref[sp, rows, :]
            d["kk"] = in_ref[4 + sp, rows, :]
            d["v"] = in_ref[6 + sp, rows, :]
            lf = in_ref[2 + sp, rows, :]
            d["b"] = _exact_dot01(tri_ref[...], lf)
            b_ref[d["slot"], B0:B0 + C, :] = d["b"]
        for d in units:
            b = d["b"]
            d["blast"] = b[C - 1:C, :]
            d["qe"] = (d["q"] * jnp.exp(b)).astype(BF16)
            kd = (d["kk"] * jnp.exp(d["blast"] - b)).astype(BF16)
            d["upd"] = _dot_tn(d["v"].astype(BF16), kd)
            d["att"] = None
            d["qb"] = d["q"].astype(BF16)
            d["kb"] = d["kk"].astype(BF16)
        for sp in range(2):
            st = st_ref[sp]
            for d in units:
                if d["sp"] == sp:
                    d["o"] = _dot_nt(d["qe"], st.astype(BF16))
                    st = st * jnp.exp(d["blast"]) + jnp.where(bd_mask, d["upd"], 0.0)
            st_ref[sp] = st
        for d in units:
            d["o"] = d["o"] + _dot((d["q"] * d["kk"]).astype(BF16), bd) * d["v"]
        for i, m in enumerate(HGRN_LEVELS):
            keep = lv_ref[i, 1] > 0.5
            for d in units:
                rm = ref_rows(b_ref.at[d["slot"]], d["b"], m)
                em = jnp.exp((d["b"] - rm) * lv_ref[i, 0]).astype(BF16)
                ke = d["kb"] * em
                ks2 = jnp.concatenate([ke * h0b, ke * h1b], axis=0)
                p = _dot_nt(d["qb"] * em, ks2)
                d["att"] = jnp.where(keep, p, 0.0 if d["att"] is None else d["att"])
        for d in units:
            v = d["v"]
            vbd = jnp.concatenate([v * h0f, v * h1f], axis=0).astype(BF16)
            d["o"] = d["o"] + _dot(d["att"].astype(BF16), vbd)
        for d in units:
            sp, rows, o = d["sp"], d["rows"], d["o"]
            o2 = o * o
            ss0 = jnp.sum(jnp.where(h0, o2, 0.0), axis=-1, keepdims=True)
            ss1 = jnp.sum(jnp.where(h0, 0.0, o2), axis=-1, keepdims=True)
            r = lax.rsqrt(jnp.where(h0, ss0, ss1) * (1.0 / HEAD_DIM) + NORM_EPS)
            out_ref[sp, rows, :] = o * r * nw_ref[sp:sp + 1, :] * in_ref[8 + sp, rows, :]
        return carry

    lax.fori_loop(0, T // (C * U), step, 0)


def _hgrn_call(hgrn_in, nw, tri, lv, bd, batch, seq):
    T = HGRN_TILE
    nt = seq // T
    n = batch * seq
    assert HGRN_CHUNK == HEAD_DIM
    return pl.pallas_call(
        _hgrn_body,
        grid=(batch, nt),
        in_specs=[
            pl.BlockSpec((N_HGRN_SLABS, T, LANES), lambda b, j: (0, b * nt + j, 0)),
            _const_spec((2, LANES)),
            _const_spec((HGRN_CHUNK, HGRN_CHUNK)),
            _const_spec((len(HGRN_LEVELS), 2, HGRN_CHUNK, LANES)),
            _const_spec((LANES, LANES)),
        ],
        out_specs=pl.BlockSpec((2, T, LANES), lambda b, j: (0, b * nt + j, 0)),
        out_shape=jax.ShapeDtypeStruct((2, n, LANES), F32),
        scratch_shapes=[
            pltpu.VMEM((2, LANES, LANES), F32),
            pltpu.VMEM((2 * HGRN_UNROLL, 8 + HGRN_CHUNK, LANES), F32),
        ],
        compiler_params=pltpu.CompilerParams(
            dimension_semantics=("arbitrary", "arbitrary"), vmem_limit_bytes=VMEM_LIMIT),
        name="hgrn2",
    )(hgrn_in, nw, tri, lv, bd)


def _mix_ffn_body(h_ref, ya_ref, ys_ref, yh_ref, wo_ref, nw_ref, wg_ref, wu_ref, wd_ref, fw_ref,
                  o_ref, a_ref, *, final):
    parts = [ya_ref[0], ya_ref[1], ys_ref[0], ys_ref[1], ys_ref[2], ys_ref[3], yh_ref[0], yh_ref[1]]
    y = jnp.concatenate([p.astype(BF16) for p in parts], axis=1)
    x = h_ref[...] + _dot(y, wo_ref[...])
    _ffn_tail(x, nw_ref, wg_ref, wu_ref, wd_ref, fw_ref, o_ref, a_ref, final)


def _mix_ffn_call(h, ya, ys, yh, wo, nw, wg, wu, wd, fw, layer, final):
    n = h.shape[0]
    tm = TOKEN_TILE
    return pl.pallas_call(
        functools.partial(_mix_ffn_body, final=final),
        grid=(n // tm,),
        in_specs=[
            pl.BlockSpec((tm, D_MODEL), lambda i: (i, 0)),
            pl.BlockSpec((2, tm, LANES), lambda i: (0, i, 0)),
            pl.BlockSpec((4, tm, LANES), lambda i: (0, i, 0)),
            pl.BlockSpec((2, tm, LANES), lambda i: (0, i, 0)),
            _layer_spec((D_MODEL, D_MODEL), layer),
            _layer_spec((1, D_MODEL), layer),
            _layer_spec((D_MODEL, D_FF), layer),
            _layer_spec((D_MODEL, D_FF), layer),
            _layer_spec((D_FF, D_MODEL), layer),
            _const_spec((1, D_MODEL)),
        ],
        out_specs=pl.BlockSpec((tm, D_MODEL), lambda i: (i, 0)),
        out_shape=jax.ShapeDtypeStruct((n, D_MODEL), F32),
        scratch_shapes=[pltpu.VMEM((tm, D_FF), BF16)],
        compiler_params=pltpu.CompilerParams(
            dimension_semantics=("arbitrary",), vmem_limit_bytes=VMEM_LIMIT),
        name="outproj_ffn",
    )(h, ya, ys, yh, wo, nw, wg, wu, wd, fw)


def _permute_w_in(w):
    o = 0
    cols = {}
    for name, size in (("aq", 256), ("ak", 256), ("av", 256), ("z", 512), ("x", 512), ("B", 256),
                       ("C", 256), ("dt", 8), ("hq", 256), ("hf", 256), ("hi", 256), ("hg", 256)):
        cols[name] = w[..., o:o + size]
        o += size
    dt = jnp.pad(cols["dt"], ((0, 0), (0, 0), (0, LANES - SSM_HEADS)))
    order = ["x", "B", "C", "z", "hq", "hf", "hg", "hi"]
    return jnp.concatenate([cols[k] for k in order] + [dt] + [cols[k] for k in ("ak", "av", "aq")], axis=-1)


def _pad_lanes(v):
    return jnp.pad(v, (0, LANES - v.shape[0])).reshape(1, LANES)


def kernel(x, ffn1_norm, ffn1_w_gate, ffn1_w_up, ffn1_w_down, mix_norm, w_in, conv_w, conv_b, dt_bias, a_log, d_skip, ssm_norm, hgrn_lb_logits, hgrn_norm, w_out, ffn2_norm, ffn2_w_gate, ffn2_w_up, ffn2_w_down, final_norm):
    batch, seq, d = x.shape
    depth = w_in.shape[0]
    assert d == D_MODEL and seq % ATTN_TILE == 0
    h = x.reshape(batch * seq, d)

    sm = jax.nn.softmax(hgrn_lb_logits.astype(F32), axis=0)
    lower = jnp.clip(jnp.cumsum(sm, axis=0) - sm[0], HGRN_LB_FLOOR, 1.0 - 1e-6)
    tri = jnp.asarray(np.tril(np.ones((SSM_CHUNK, SSM_CHUNK), np.float32)), BF16)
    tri_h = jnp.asarray(np.tril(np.ones((HGRN_CHUNK, HGRN_CHUNK), np.float32)), BF16)
    lv = jnp.asarray(_hgrn_level_consts())
    bd = jnp.asarray(np.kron(np.eye(2, dtype=np.float32), np.ones((HEAD_DIM, HEAD_DIM), np.float32)), BF16)
    fw = final_norm.reshape(1, d)
    attn_bias = jnp.asarray(_attn_bias())

    norms = [w.reshape(depth, 1, d) for w in (ffn1_norm, mix_norm, ffn2_norm)]
    ffn1_w = [w.astype(BF16) for w in (ffn1_w_gate, ffn1_w_up, ffn1_w_down)]
    ffn2_w = [w.astype(BF16) for w in (ffn2_w_gate, ffn2_w_up, ffn2_w_down)]
    w_in_b = _permute_w_in(w_in).astype(BF16)
    w_out_b = w_out.astype(BF16)

    for layer in range(depth):
        h = _ffn_call(h, norms[0], *ffn1_w, fw, layer, False)
        lb = lower[layer]
        attn_in, ssd_in, hgrn_in = _inproj_call(
            h, norms[1], w_in_b, conv_w[layer].reshape(SSM_CONV, 8, LANES), conv_b[layer].reshape(8, LANES),
            _pad_lanes(dt_bias[layer]), jnp.log(lb).reshape(2, LANES), jnp.log1p(-lb).reshape(2, LANES),
            layer, seq)
        y_attn = _attn_call(attn_in, attn_bias, batch, seq)
        y_ssm = _ssd_call(ssd_in, _pad_lanes(a_log[layer]), jnp.repeat(d_skip[layer], HEAD_DIM).reshape(4, LANES),
                          ssm_norm[layer].reshape(4, LANES), tri, batch, seq)
        y_hgrn = _hgrn_call(hgrn_in, hgrn_norm[layer].reshape(2, LANES), tri_h, lv, bd, batch, seq)
        h = _mix_ffn_call(h, y_attn, y_ssm, y_hgrn, w_out_b, norms[2], *ffn2_w, fw, layer,
                          layer == depth - 1)
    return h.reshape(batch, seq, d)
```
